```python
import jax
import jax.numpy as jnp
from jax import lax
import numpy as np

D_MODEL = 4096
BATCH = 1
SEQ = 8192
DEPTH = 2
DEC_BATCH = 16
DEC_SEQ = 32
PAST_LEN = 1024

CHUNK = 64
MIX_WIDTH = 2048
HEAD_DIM = 64
RWKV_HEADS = MIX_WIDTH // HEAD_DIM
DECAY_LORA = 96
ICLR_LORA = 96
GATE_LORA = 256
GN_EPS = 64e-5
CONV_WIDTH = 3
Q_HEADS = MIX_WIDTH // HEAD_DIM
KV_HEADS = 8
GROUP = Q_HEADS // KV_HEADS
WINDOW = 128
W_CHUNKS = WINDOW // CHUNK
ATTN_SCALE = HEAD_DIM ** -0.5
N_BRANCH = 3
D_FF = 14336
N_EXPERTS = 8
TOP_K = 2
RMS_EPS = 1e-5
RWKV_COLS = 3 * MIX_WIDTH + DECAY_LORA + ICLR_LORA + GATE_LORA
CONV_COLS = 3 * MIX_WIDTH
ATTN_COLS = (Q_HEADS + 2 * KV_HEADS) * HEAD_DIM
GATE_COLS = N_BRANCH * D_MODEL
IN_COLS = RWKV_COLS + CONV_COLS + ATTN_COLS + GATE_COLS

kernel_name = 'hybrid_rwkv7_conv_swa_streaming_step'


def split_cols(z, sizes):
    out, off = [], 0
    for s in sizes:
        out.append(z[..., off:off + s])
        off += s
    return out


def rms_norm(x, g):
    xf = x.astype(jnp.float32)
    y = xf * lax.rsqrt(jnp.mean(xf * xf, axis=-1, keepdims=True) + RMS_EPS)
    return (y * g.astype(jnp.float32)).astype(x.dtype)


def swiglu(h, wg, wu, wd):
    return (jax.nn.silu(h @ wg) * (h @ wu)) @ wd


def moe_swiglu(h, router, wg, wu, wd):
    logits = (h @ router).astype(jnp.float32)
    top_v, top_i = lax.top_k(logits, TOP_K)
    w_top = jax.nn.softmax(top_v, axis=-1)
    gate = jnp.sum(jax.nn.one_hot(top_i, N_EXPERTS, dtype=jnp.float32) * w_top[..., None], axis=-2)
    gate = gate.astype(h.dtype)
    out = jnp.zeros_like(h)
    for e in range(N_EXPERTS):
        out = out + gate[..., e:e + 1] * swiglu(h, wg[e], wu[e], wd[e])
    return out


def wkv7_scan(S0, r, w, k, v, a, b):
    def step(S, inp):
        r_t, w_t, k_t, v_t, a_t, b_t = inp
        sa = jnp.einsum('bhvk,bhk->bhv', S, a_t)
        S = S * w_t[:, :, None, :] + sa[..., None] * b_t[:, :, None, :] + v_t[..., None] * k_t[:, :, None, :]
        y = jnp.einsum('bhvk,bhk->bhv', S, r_t)
        return S, y
    xs = tuple(jnp.swapaxes(t.astype(jnp.float32), 0, 1) for t in (r, w, k, v, a, b))
    S, ys = lax.scan(step, S0.astype(jnp.float32), xs)
    return S, jnp.swapaxes(ys, 0, 1)


def rwkv_branch(z, z_prev, S0, mu, w_up, w0, a_up, a0, g_up, k_k, k_a, r_k, gn_w, gn_b):
    B, T, _ = z.shape
    z_shift = jnp.concatenate([z_prev.astype(z.dtype), z[:, :-1]], axis=1)
    z = z + (z_shift - z) * mu
    r, k, v, w_lo, a_lo, g_lo = split_cols(z, (MIX_WIDTH, MIX_WIDTH, MIX_WIDTH, DECAY_LORA, ICLR_LORA, GATE_LORA))
    w_raw = (w0 + jnp.tanh(w_lo) @ w_up).astype(jnp.float32)
    w = jnp.exp(-jnp.exp(-jax.nn.softplus(-w_raw) - 0.5))
    a = jax.nn.sigmoid(a0 + a_lo @ a_up).astype(jnp.float32)
    g = jax.nn.sigmoid(g_lo) @ g_up

    def heads(t):
        return t.astype(jnp.float32).reshape(B, T, RWKV_HEADS, HEAD_DIM)

    kk = heads(k * k_k)
    kk = kk / jnp.maximum(jnp.linalg.norm(kk, axis=-1, keepdims=True), 1e-12)
    k_t = k.astype(jnp.float32) * (1 + (a - 1) * k_a)
    rh, kh, vh, wh, ah = heads(r), heads(k_t), heads(v), heads(w), heads(a)
    S, y = wkv7_scan(S0, rh, wh, kh, vh, -kk, kk * ah)
    mean = jnp.mean(y, axis=-1, keepdims=True)
    var = jnp.mean(jnp.square(y - mean), axis=-1, keepdims=True)
    y = (y - mean) * lax.rsqrt(var + GN_EPS) * gn_w.reshape(RWKV_HEADS, HEAD_DIM) + gn_b.reshape(RWKV_HEADS, HEAD_DIM)
    y = y + jnp.sum(rh * kh * r_k.reshape(RWKV_HEADS, HEAD_DIM), axis=-1, keepdims=True) * vh
    out = y.reshape(B, T, MIX_WIDTH).astype(z.dtype) * g
    return out, S


def conv_branch(zc, buf, conv_w):
    bg, cg, xin = split_cols(zc, (MIX_WIDTH, MIX_WIDTH, MIX_WIDTH))
    u = cg * xin
    up = jnp.concatenate([buf.astype(u.dtype), u], axis=1)
    T = u.shape[1]
    y = conv_w[0] * up[:, 0:T]
    for j in range(1, CONV_WIDTH):
        y = y + conv_w[j] * up[:, j:j + T]
    return bg * y, up[:, -(CONV_WIDTH - 1):]


def alibi_slopes():
    return jnp.exp2(-8.0 * jnp.arange(1, Q_HEADS + 1, dtype=jnp.float32) / Q_HEADS).reshape(KV_HEADS, GROUP)


def band_bias(qpos, kpos, slopes):
    dc = qpos[..., :, None] // CHUNK - kpos[..., None, :] // CHUNK
    valid = (dc >= 0) & (dc <= W_CHUNKS) & (kpos[..., None, :] >= 0)
    dist = jnp.abs(qpos[..., :, None] - kpos[..., None, :]).astype(jnp.float32)
    bias = -slopes[:, :, None, None] * dist[..., None, None, :, :]
    return jnp.where(valid[..., None, None, :, :], bias, -jnp.inf)


def sink_softmax(s, sinks):
    sk = sinks.astype(jnp.float32).reshape(KV_HEADS, GROUP)[..., None, None]
    m = jnp.maximum(jnp.max(s, axis=-1, keepdims=True), sk)
    p = jnp.exp(s - m)
    return p / (jnp.sum(p, axis=-1, keepdims=True) + jnp.exp(sk - m))


def swa_prompt(q, k, v, sinks):
    B, T = q.shape[:2]
    nC = T // CHUNK
    KB = (W_CHUNKS + 1) * CHUNK
    qc = q.reshape(B, nC, CHUNK, KV_HEADS, GROUP, HEAD_DIM)

    def band(t):
        tp = jnp.pad(t, ((0, 0), (W_CHUNKS * CHUNK, 0), (0, 0), (0, 0)))
        tp = tp.reshape(B, nC + W_CHUNKS, CHUNK, KV_HEADS, HEAD_DIM)
        return jnp.concatenate([tp[:, j:j + nC] for j in range(W_CHUNKS + 1)], axis=2)

    kb, vb = band(k), band(v)
    qpos = jnp.arange(T, dtype=jnp.int32).reshape(nC, CHUNK)
    kpos = (jnp.arange(nC, dtype=jnp.int32)[:, None] - W_CHUNKS) * CHUNK + jnp.arange(KB, dtype=jnp.int32)[None, :]
    s = jnp.einsum('bnqhgd,bnkhd->bnhgqk', qc, kb).astype(jnp.float32) * ATTN_SCALE
    s = s + band_bias(qpos, kpos, alibi_slopes())
    p = sink_softmax(s, sinks)
    o = jnp.einsum('bnhgqk,bnkhd->bnqhgd', p.astype(v.dtype), vb)
    return o.reshape(B, T, Q_HEADS * HEAD_DIM)


def swa_sample(q, k, v, k_cache, v_cache, sinks, pos0):
    B, S = q.shape[:2]
    L = k_cache.shape[1]
    ka = jnp.concatenate([k_cache.astype(k.dtype), k], axis=1)
    va = jnp.concatenate([v_cache.astype(v.dtype), v], axis=1)
    qpos = pos0 + jnp.arange(S, dtype=jnp.int32)
    kpos = pos0 - L + jnp.arange(L + S, dtype=jnp.int32)
    qg = q.reshape(B, S, KV_HEADS, GROUP, HEAD_DIM)
    s = jnp.einsum('bqhgd,bkhd->bhgqk', qg, ka).astype(jnp.float32) * ATTN_SCALE
    s = s + band_bias(qpos, kpos, alibi_slopes())
    p = sink_softmax(s, sinks)
    o = jnp.einsum('bhgqk,bkhd->bqhgd', p.astype(v.dtype), va)
    return o.reshape(B, S, Q_HEADS * HEAD_DIM), ka[:, -L:], va[:, -L:]


def token_mixers(h, S0, shift0, conv0, k_cache, v_cache, pos0, w_in, mu, w_up, w0, a_up, a0, g_up,
                 k_k, k_a, r_k, gn_w, gn_b, conv_w, sinks, w_branch_out, w_mix_out):
    B, T, _ = h.shape
    z = h @ w_in
    z_r, z_c, z_a, z_g = split_cols(z, (RWKV_COLS, CONV_COLS, ATTN_COLS, GATE_COLS))
    y_a, S_new = rwkv_branch(z_r, shift0, S0, mu, w_up, w0, a_up, a0, g_up, k_k, k_a, r_k, gn_w, gn_b)
    shift_new = z_r[:, -1:]
    y_b, conv_new = conv_branch(z_c, conv0, conv_w)
    q, k, v = split_cols(z_a, (Q_HEADS * HEAD_DIM, KV_HEADS * HEAD_DIM, KV_HEADS * HEAD_DIM))
    q = q.reshape(B, T, Q_HEADS, HEAD_DIM)
    k = k.reshape(B, T, KV_HEADS, HEAD_DIM)
    v = v.reshape(B, T, KV_HEADS, HEAD_DIM)
    if k_cache is None:
        y_c = swa_prompt(q, k, v, sinks)
        k_new, v_new = k[:, -WINDOW:], v[:, -WINDOW:]
    else:
        y_c, k_new, v_new = swa_sample(q, k, v, k_cache, v_cache, sinks, pos0)
    gates = jax.nn.sigmoid(z_g.reshape(B, T, N_BRANCH, D_MODEL))
    branches = jnp.stack([y_a.astype(h.dtype), y_b.astype(h.dtype), y_c.astype(h.dtype)], axis=2)
    proj = jnp.einsum('btnw,nwd->btnd', branches, w_branch_out)
    merged = jnp.sum(gates * proj, axis=2)
    return merged @ w_mix_out, (S_new, shift_new, conv_new, k_new, v_new)


def run_trunk(x, c, states, pos0, weights):
    (ada_w, ada_b, norm_mix_g, norm_ffn_g, w_in, rwkv_mu, rwkv_w_up, rwkv_w0, rwkv_a_up, rwkv_a0,
     rwkv_g_up, rwkv_k_k, rwkv_k_a, rwkv_r_k, rwkv_gn_w, rwkv_gn_b, conv_w, attn_sinks,
     w_branch_out, w_mix_out, ffn_w_gate, ffn_w_up, ffn_w_down, moe_router, moe_w_gate,
     moe_w_up, moe_w_down, final_norm_g) = weights
    B = x.shape[0]
    collected = ([], [], [], [], [])
    for i in range(DEPTH):
        if states is None:
            st = (jnp.zeros((B, RWKV_HEADS, HEAD_DIM, HEAD_DIM), jnp.float32),
                  jnp.zeros((B, 1, RWKV_COLS), x.dtype),
                  jnp.zeros((B, CONV_WIDTH - 1, MIX_WIDTH), x.dtype), None, None)
        else:
            st = tuple(s[i] for s in states)
        mod = (jax.nn.silu(c) @ ada_w[i] + ada_b[i]).reshape(B, 6, 1, D_MODEL)
        h = rms_norm(x, norm_mix_g[i]) * (1 + mod[:, 1]) + mod[:, 0]
        mix, st_new = token_mixers(h, st[0], st[1], st[2], st[3], st[4], pos0, w_in[i], rwkv_mu[i],
                                   rwkv_w_up[i], rwkv_w0[i], rwkv_a_up[i], rwkv_a0[i], rwkv_g_up[i],
                                   rwkv_k_k[i], rwkv_k_a[i], rwkv_r_k[i], rwkv_gn_w[i], rwkv_gn_b[i],
                                   conv_w[i], attn_sinks[i], w_branch_out[i], w_mix_out[i])
        x = x + mod[:, 2] * mix
        h = rms_norm(x, norm_ffn_g[i]) * (1 + mod[:, 4]) + mod[:, 3]
        j = i // 2
        if i % 2 == 0:
            f = swiglu(h, ffn_w_gate[j], ffn_w_up[j], ffn_w_down[j])
        else:
            f = moe_swiglu(h, moe_router[j], moe_w_gate[j], moe_w_up[j], moe_w_down[j])
        x = x + mod[:, 5] * f
        for lst, s in zip(collected, st_new):
            lst.append(s)
    y = rms_norm(x, final_norm_g)
    return y, tuple(jnp.stack(lst) for lst in collected)


def setup_inputs(seed: int = 0) -> dict:
    key = jax.random.key(seed)
    ks = iter(jax.random.split(key, 64))

    def nrm(shape, scale):
        return scale * jax.random.normal(next(ks), shape, jnp.float32)

    n_dense = (DEPTH + 1) // 2
    n_moe = DEPTH // 2
    swa_len = min(WINDOW, PAST_LEN)
    D, W = D_MODEL, MIX_WIDTH
    return {
        'x_prompt': nrm((BATCH, SEQ, D), 1.0),
        'x_sample': nrm((DEC_BATCH, DEC_SEQ, D), 1.0),
        'c_prompt': nrm((BATCH, D), 1.0),
        'c_sample': nrm((DEC_BATCH, D), 1.0),
        'state_rwkv': nrm((DEPTH, DEC_BATCH, RWKV_HEADS, HEAD_DIM, HEAD_DIM), 0.5),
        'state_rwkv_shift': nrm((DEPTH, DEC_BATCH, 1, RWKV_COLS), 1.0),
        'state_conv': nrm((DEPTH, DEC_BATCH, CONV_WIDTH - 1, W), 1.0),
        'cache_swa_k': nrm((DEPTH, DEC_BATCH, swa_len, KV_HEADS, HEAD_DIM), 1.0),
        'cache_swa_v': nrm((DEPTH, DEC_BATCH, swa_len, KV_HEADS, HEAD_DIM), 1.0),
        'ada_w': nrm((DEPTH, D, 6 * D), 0.5 * D ** -0.5),
        'ada_b': nrm((DEPTH, 6 * D), 0.02),
        'norm_mix_g': 1.0 + nrm((DEPTH, D), 0.05),
        'norm_ffn_g': 1.0 + nrm((DEPTH, D), 0.05),
        'w_in': nrm((DEPTH, D, IN_COLS), D ** -0.5),
        'rwkv_mu': jax.random.uniform(next(ks), (DEPTH, RWKV_COLS), jnp.float32),
        'rwkv_w_up': nrm((DEPTH, DECAY_LORA, W), DECAY_LORA ** -0.5),
        'rwkv_w0': nrm((DEPTH, W), 0.5),
        'rwkv_a_up': nrm((DEPTH, ICLR_LORA, W), ICLR_LORA ** -0.5),
        'rwkv_a0': nrm((DEPTH, W), 0.5),
        'rwkv_g_up': nrm((DEPTH, GATE_LORA, W), GATE_LORA ** -0.5),
        'rwkv_k_k': 0.85 + nrm((DEPTH, W), 0.1),
        'rwkv_k_a': 1.0 + nrm((DEPTH, W), 0.1),
        'rwkv_r_k': nrm((DEPTH, W), 0.1),
        'rwkv_gn_w': 1.0 + nrm((DEPTH, W), 0.05),
        'rwkv_gn_b': nrm((DEPTH, W), 0.02),
        'conv_w': nrm((DEPTH, CONV_WIDTH, W), CONV_WIDTH ** -0.5),
        'attn_sinks': nrm((DEPTH, Q_HEADS), 1.0),
        'w_branch_out': nrm((DEPTH, N_BRANCH, W, D), W ** -0.5),
        'w_mix_out': nrm((DEPTH, D, D), D ** -0.5),
        'ffn_w_gate': nrm((n_dense, D, D_FF), D ** -0.5),
        'ffn_w_up': nrm((n_dense, D, D_FF), D ** -0.5),
        'ffn_w_down': nrm((n_dense, D_FF, D), D_FF ** -0.5),
        'moe_router': nrm((n_moe, D, N_EXPERTS), D ** -0.5),
        'moe_w_gate': nrm((n_moe, N_EXPERTS, D, D_FF), D ** -0.5),
        'moe_w_up': nrm((n_moe, N_EXPERTS, D, D_FF), D ** -0.5),
        'moe_w_down': nrm((n_moe, N_EXPERTS, D_FF, D), D_FF ** -0.5),
        'final_norm_g': 1.0 + nrm((D,), 0.05),
    }


def reference(x_prompt, x_sample, c_prompt, c_sample, state_rwkv, state_rwkv_shift, state_conv,
              cache_swa_k, cache_swa_v, ada_w, ada_b, norm_mix_g, norm_ffn_g, w_in, rwkv_mu,
              rwkv_w_up, rwkv_w0, rwkv_a_up, rwkv_a0, rwkv_g_up, rwkv_k_k, rwkv_k_a, rwkv_r_k,
              rwkv_gn_w, rwkv_gn_b, conv_w, attn_sinks, w_branch_out, w_mix_out, ffn_w_gate,
              ffn_w_up, ffn_w_down, moe_router, moe_w_gate, moe_w_up, moe_w_down, final_norm_g):
    weights = (ada_w, ada_b, norm_mix_g, norm_ffn_g, w_in, rwkv_mu, rwkv_w_up, rwkv_w0, rwkv_a_up,
               rwkv_a0, rwkv_g_up, rwkv_k_k, rwkv_k_a, rwkv_r_k, rwkv_gn_w, rwkv_gn_b, conv_w,
               attn_sinks, w_branch_out, w_mix_out, ffn_w_gate, ffn_w_up, ffn_w_down, moe_router,
               moe_w_gate, moe_w_up, moe_w_down, final_norm_g)
    y_prompt, (p_rwkv, p_shift, p_conv, p_k, p_v) = run_trunk(x_prompt, c_prompt, None, 0, weights)
    sample_states = (state_rwkv, state_rwkv_shift, state_conv, cache_swa_k, cache_swa_v)
    y_sample, (s_rwkv, s_shift, s_conv, s_k, s_v) = run_trunk(x_sample, c_sample, sample_states, PAST_LEN, weights)
    return (y_prompt, y_sample, p_rwkv, p_shift, p_conv, p_k, p_v, s_rwkv, s_shift, s_conv, s_k, s_v)
```

```python
import functools
import math

import jax
import jax.numpy as jnp
from jax import lax
from jax.experimental import pallas as pl
from jax.experimental.pallas import tpu as pltpu

F32 = jnp.float32
BF16 = jnp.bfloat16

HEAD_DIM = 64
N_HEADS = 32
MIX_WIDTH = N_HEADS * HEAD_DIM
KV_HEADS = 8
GROUP = N_HEADS // KV_HEADS
KV_WIDTH = KV_HEADS * HEAD_DIM
CHUNK = 64
CHUNK_SHIFT = 6
W_CHUNKS = 2
WINDOW = W_CHUNKS * CHUNK
PAST_LEN = 1024
CONV_WIDTH = 3
DECAY_LORA = 96
ICLR_LORA = 96
GATE_LORA = 256
N_BRANCH = 3
TOP_K = 2
RMS_EPS = 1e-5
GN_EPS = 64e-5
ATTN_SCALE = HEAD_DIM ** -0.5

LANES = 128
SUBLANES = 8
VMEM_LIMIT_BYTES = 56 * 1024 * 1024

LORA_PAD = 512
OFF_R, OFF_K, OFF_V = 0, MIX_WIDTH, 2 * MIX_WIDTH
OFF_LORA = 3 * MIX_WIDTH
RWKV_PAD = OFF_LORA + LORA_PAD
OFF_CONV = RWKV_PAD
OFF_ATTN = OFF_CONV + 3 * MIX_WIDTH
OFF_GATE = OFF_ATTN + MIX_WIDTH + 2 * KV_WIDTH
COL_BLK = 512
RWKV_COLS = 3 * MIX_WIDTH + DECAY_LORA + ICLR_LORA + GATE_LORA


def _cparams(sem):
    return pltpu.CompilerParams(dimension_semantics=sem, vmem_limit_bytes=VMEM_LIMIT_BYTES)


def _pick(n, cands):
    for c in cands:
        if n % c == 0:
            return c
    return n


def _row_tiles(B, S, tm_max):
    if S >= tm_max:
        ts = _pick(S, (tm_max, tm_max // 2, tm_max // 4, 256, 128, 64, 32, 16, 8))
        return 1, ts
    bb = max(1, min(B, tm_max // S))
    while B % bb:
        bb -= 1
    return bb, S


def _mm_kernel(*refs, n_w, epi, tm, silu_in, e_col):
    x_ref = refs[0]
    w_refs = refs[1:1 + n_w]
    pos = 1 + n_w
    extra = ()
    n_extra = {"none": 0, "swiglu": 0, "bias": 1, "resgate": 2, "resgate_rs": 3}[epi]
    extra = refs[pos:pos + n_extra]
    pos += n_extra
    o_ref = refs[pos]
    acc_refs = refs[pos + 1:]
    direct = not acc_refs
    if direct:
        acc_refs = (o_ref,)
    k = pl.program_id(3)
    nk = pl.num_programs(3)

    x = x_ref[...]
    x = x.reshape(tm, x.shape[-1])
    if silu_in:
        xf = x.astype(F32)
        x = xf * jax.nn.sigmoid(xf)
    x = x.astype(BF16)
    for j in range(n_w):
        p = jnp.dot(x, w_refs[j][...].astype(BF16), preferred_element_type=F32)

        p = p.reshape(acc_refs[j].shape)

        @pl.when(k == 0)
        def _(p=p, j=j):
            acc_refs[j][...] = p

        @pl.when(k != 0)
        def _(p=p, j=j):
            acc_refs[j][...] += p

    @pl.when(k == nk - 1)
    def _():
        if direct and epi == "none":
            return
        a = acc_refs[0][...]
        if epi == "swiglu":
            r = (a * jax.nn.sigmoid(a)) * acc_refs[1][...]
        elif epi == "bias":
            r = a + extra[0][...].reshape((1,) * (a.ndim - 1) + (-1,))
        elif epi in ("resgate", "resgate_rs"):
            a3 = a.reshape(o_ref.shape)
            if epi == "resgate_rs":
                a3 = a3 * extra[2][...][:, :, e_col:e_col + 1]
            r = extra[0][...] + extra[1][...] * a3
        else:
            r = a
        o_ref[...] = r.reshape(o_ref.shape).astype(o_ref.dtype)


def _mm(x, ws, leads, *, epi="none", out_dtype=F32, extras=(), silu_in=False, e_col=None,
        tm_max=2048, name="mm"):
    B, S, K = x.shape
    N = ws[0].shape[-1]
    bb, ts = _row_tiles(B, S, tm_max)
    tm = bb * ts
    tn = _pick(N, (1280, 1024, 512, 256, 128))
    tk = _pick(K, (512, 256, 128))
    grid = (B // bb, S // ts, N // tn, K // tk)

    in_specs = [pl.BlockSpec((bb, ts, tk), lambda b, m, n, k: (b, m, k))]
    for w, lead in zip(ws, leads):
        nl = len(lead)
        in_specs.append(pl.BlockSpec((None,) * nl + (tk, tn),
                                     lambda b, m, n, k, lead=lead: tuple(lead) + (k, n)))
    if epi == "bias":
        in_specs.append(pl.BlockSpec((None, 1, tn), lambda b, m, n, k, l=extras[1]: (l, 0, n)))
        extras = extras[:1]
    elif epi in ("resgate", "resgate_rs"):
        in_specs.append(pl.BlockSpec((bb, ts, tn), lambda b, m, n, k: (b, m, n)))
        in_specs.append(pl.BlockSpec((bb, 1, tn), lambda b, m, n, k: (b, 0, n)))
        if epi == "resgate_rs":
            in_specs.append(pl.BlockSpec((bb, ts, LANES), lambda b, m, n, k: (b, m, 0)))
    n_w = len(ws)
    direct = n_w == 1 and out_dtype == F32
    kern = functools.partial(_mm_kernel, n_w=n_w, epi=epi, tm=tm, silu_in=silu_in, e_col=e_col)
    return pl.pallas_call(
        kern,
        out_shape=jax.ShapeDtypeStruct((B, S, N), out_dtype),
        grid=grid,
        in_specs=in_specs,
        out_specs=pl.BlockSpec((bb, ts, tn), lambda b, m, n, k: (b, m, n)),
        scratch_shapes=[] if direct else [pltpu.VMEM((tm, tn), F32) for _ in range(n_w)],
        compiler_params=_cparams(("parallel", "parallel", "parallel", "arbitrary")),
        name=name,
    )(x, *ws, *extras)


def _norm_kernel(*refs, modulate, route):
    x_ref, g_ref = refs[0], refs[1]
    pos = 2
    if modulate:
        sc_ref, sh_ref = refs[2], refs[3]
        pos = 4
    if route:
        r_ref = refs[pos]
        pos += 1
    o_ref = refs[pos]
    x = x_ref[...]
    ms = jnp.mean(x * x, axis=-1, keepdims=True)
    y = x * lax.rsqrt(ms + RMS_EPS) * g_ref[...]
    if modulate:
        y = y * (1.0 + sc_ref[...]) + sh_ref[...]
    o_ref[...] = y.astype(o_ref.dtype)
    if route:
        l_ref = refs[pos + 1]
        y2 = y.reshape(-1, y.shape[-1])
        logits = jnp.dot(y2, r_ref[...], preferred_element_type=F32, precision=lax.Precision.HIGHEST)
        l_ref[...] = logits.reshape(l_ref.shape)


def _norm(x, g3, lead, scale=None, shift=None, router=None, out_dtype=BF16, name="norm"):
    B, S, D = x.shape
    bb, ts = _row_tiles(B, S, 256)
    grid = (B // bb, S // ts)
    modulate = scale is not None
    route = router is not None
    in_specs = [pl.BlockSpec((bb, ts, D), lambda b, m: (b, m, 0)),
                pl.BlockSpec((None, 1, D), lambda b, m: (lead, 0, 0))]
    args = [x, g3]
    if modulate:
        in_specs += [pl.BlockSpec((bb, 1, D), lambda b, m: (b, 0, 0))] * 2
        args += [scale, shift]
    out_shape = [jax.ShapeDtypeStruct((B, S, D), out_dtype)]
    out_specs = [pl.BlockSpec((bb, ts, D), lambda b, m: (b, m, 0))]
    if route:
        in_specs.append(pl.BlockSpec((D, LANES), lambda b, m: (0, 0)))
        args.append(router)
        out_shape.append(jax.ShapeDtypeStruct((B, S, LANES), F32))
        out_specs.append(pl.BlockSpec((bb, ts, LANES), lambda b, m: (b, m, 0)))
    res = pl.pallas_call(
        functools.partial(_norm_kernel, modulate=modulate, route=route),
        out_shape=out_shape, grid=grid, in_specs=in_specs, out_specs=out_specs,
        compiler_params=_cparams(("parallel", "parallel")), name=name,
    )(*args)
    return res if route else res[0]


def _topk_kernel(l_ref, g_ref, *, n_exp):
    l = l_ref[...]
    lane = lax.broadcasted_iota(jnp.int32, l.shape, l.ndim - 1)
    neg = jnp.float32(-jnp.inf)
    l = jnp.where(lane < n_exp, l, neg)
    m1 = jnp.max(l, axis=-1, keepdims=True)
    i1 = jnp.min(jnp.where(l == m1, lane, LANES), axis=-1, keepdims=True)
    l2 = jnp.where(lane == i1, neg, l)
    m2 = jnp.max(l2, axis=-1, keepdims=True)
    i2 = jnp.min(jnp.where(l2 == m2, lane, LANES), axis=-1, keepdims=True)
    e = jnp.exp(m2 - m1)
    den = 1.0 + e
    g_ref[...] = jnp.where(lane == i1, 1.0 / den, 0.0) + jnp.where(lane == i2, e / den, 0.0)


def _topk_gate(logits, n_exp):
    B, S, _ = logits.shape
    bb, ts = _row_tiles(B, S, 1024)
    spec = pl.BlockSpec((bb, ts, LANES), lambda b, m: (b, m, 0))
    return pl.pallas_call(
        functools.partial(_topk_kernel, n_exp=n_exp),
        out_shape=jax.ShapeDtypeStruct(logits.shape, F32), grid=(B // bb, S // ts),
        in_specs=[spec], out_specs=spec,
        compiler_params=_cparams(("parallel", "parallel")), name="topk_gate",
    )(logits)


def _head_sum(x):
    s = x[:, 0:LANES]
    for i in range(1, MIX_WIDTH // LANES):
        s = s + x[:, i * LANES:(i + 1) * LANES]
    s = s + pltpu.roll(s, 64, axis=1)
    s = s + pltpu.roll(s, 32, axis=1)
    return s


def _tile16(s):
    return jnp.concatenate([s] * (MIX_WIDTH // LANES), axis=1)


def _rwkv_prep_kernel(z_ref, pz_ref, sh0_ref, mu_ref, wup_ref, w0_ref, aup_ref, a0_ref, gup_ref,
                      kk_ref, ka_ref, rk_ref,
                      a_out, wr_out, w_out, b_out, k_out, v_out, g_out, br_out, kr_out, rkr_out):
    s = pl.program_id(1)
    z = z_ref[0]
    ts = z.shape[0]
    prev = jnp.where(s == 0, sh0_ref[0], pz_ref[0][SUBLANES - 1:SUBLANES])
    rid = lax.broadcasted_iota(jnp.int32, z.shape, 0)
    zs = jnp.where(rid == 0, prev, pltpu.roll(z, 1, axis=0))
    zz = z + (zs - z) * mu_ref[0]
    r = zz[:, OFF_R:OFF_R + MIX_WIDTH]
    k = zz[:, OFF_K:OFF_K + MIX_WIDTH]
    v = zz[:, OFF_V:OFF_V + MIX_WIDTH]
    wl = jnp.tanh(zz[:, OFF_LORA:OFF_LORA + LANES])
    al = zz[:, OFF_LORA + LANES:OFF_LORA + 2 * LANES]
    gl = jax.nn.sigmoid(zz[:, OFF_LORA + 2 * LANES:OFF_LORA + 2 * LANES + GATE_LORA])

    w_raw = w0_ref[0] + jnp.dot(wl.astype(BF16), wup_ref[0].astype(BF16), preferred_element_type=F32)
    nx = -w_raw
    softplus = jnp.maximum(nx, 0.0) + jnp.log(1.0 + jnp.exp(-jnp.abs(nx)))
    w = jnp.exp(-jnp.exp(-softplus - 0.5))
    a = jax.nn.sigmoid(a0_ref[0] + jnp.dot(al.astype(BF16), aup_ref[0].astype(BF16),
                                           preferred_element_type=F32))
    g = jnp.dot(gl.astype(BF16), gup_ref[0].astype(BF16), preferred_element_type=F32)

    kk = k * kk_ref[0]
    nrm = jnp.sqrt(_head_sum(kk * kk))
    kk = kk * _tile16(1.0 / jnp.maximum(nrm, 1e-12))
    kt = k * (1.0 + (a - 1.0) * ka_ref[0])
    bs = kk * a

    a_out[0] = -kk
    wr_out[0] = w * r
    w_out[0] = w
    b_out[0] = bs
    k_out[0] = kt
    v_out[0] = v
    g_out[0] = g
    br_out[0] = _head_sum(bs * r)
    kr_out[0] = _head_sum(kt * r)
    rkr_out[0] = _head_sum(r * kt * rk_ref[0])


def _rwkv_prep(z_all, shift0, lw):
    B, S, NP = z_all.shape
    ts = _pick(S, (128, 64, 32, 16, 8))
    grid = (B, S // ts)
    nb8 = ts // SUBLANES
    vec = lambda n: pl.BlockSpec((1, 1, n), lambda b, s: (0, 0, 0))
    mat = lambda r, c: pl.BlockSpec((1, r, c), lambda b, s: (0, 0, 0))
    in_specs = [
        pl.BlockSpec((1, ts, RWKV_PAD), lambda b, s: (b, s, 0)),
        pl.BlockSpec((1, SUBLANES, RWKV_PAD), lambda b, s: (b, jnp.maximum(s * nb8 - 1, 0), 0)),
        pl.BlockSpec((1, 1, RWKV_PAD), lambda b, s: (b, 0, 0)),
        vec(RWKV_PAD),
        mat(LANES, MIX_WIDTH), vec(MIX_WIDTH),
        mat(LANES, MIX_WIDTH), vec(MIX_WIDTH),
        mat(GATE_LORA, MIX_WIDTH),
        vec(MIX_WIDTH), vec(MIX_WIDTH), vec(MIX_WIDTH),
    ]
    big = pl.BlockSpec((1, ts, MIX_WIDTH), lambda b, s: (b, s, 0))
    small = pl.BlockSpec((1, ts, LANES), lambda b, s: (b, s, 0))
    out_shape = [jax.ShapeDtypeStruct((B, S, MIX_WIDTH), F32)] * 7 + \
                [jax.ShapeDtypeStruct((B, S, LANES), F32)] * 3
    return pl.pallas_call(
        _rwkv_prep_kernel, out_shape=out_shape, grid=grid, in_specs=in_specs,
        out_specs=[big] * 7 + [small] * 3,
        compiler_params=_cparams(("parallel", "arbitrary")), name="rwkv_prep",
    )(z_all, z_all, shift0, lw["mu"], lw["w_up"], lw["w0"], lw["a_up"], lw["a0"], lw["g_up"],
      lw["k_k"], lw["k_a"], lw["r_k"])


N_VH = HEAD_DIM // SUBLANES
N_KH = MIX_WIDTH // LANES
N_STATE_TILES = N_VH * N_KH


def _lane_group_sum(x):
    x = x + pltpu.roll(x, 64, axis=1)
    return x + pltpu.roll(x, 32, axis=1)


def _rwkv_scan_kernel(a_ref, wr_ref, w_ref, b_ref, k_ref, v_ref, br_ref, kr_ref, s0_ref,
                      y_ref, st_ref, S, *, tc):
    c = pl.program_id(1)

    @pl.when(c == 0)
    def _():
        S[...] = s0_ref[0]

    def step(t, carry):
        a2 = a_ref[0, t]
        wr2 = wr_ref[0, t]
        w2 = w_ref[0, t]
        b2 = b_ref[0, t]
        k2 = k_ref[0, t]
        brr = br_ref[0, t]
        krr = kr_ref[0, t]
        for vh in range(N_VH):
            hs = [S[vh * N_KH + kh] for kh in range(N_KH)]
            au = hs[0] * a2[0:1]
            ao = hs[0] * wr2[0:1]
            for kh in range(1, N_KH):
                au = au + hs[kh] * a2[kh:kh + 1]
                ao = ao + hs[kh] * wr2[kh:kh + 1]
            u = _lane_group_sum(au)
            o = _lane_group_sum(ao)
            vv = v_ref[0, t, vh * SUBLANES:(vh + 1) * SUBLANES, :]
            y_ref[0, t, vh * SUBLANES:(vh + 1) * SUBLANES, :] = o + u * brr + vv * krr
            for kh in range(N_KH):
                S[vh * N_KH + kh] = hs[kh] * w2[kh:kh + 1] + b2[kh:kh + 1] * u + k2[kh:kh + 1] * vv
        return carry

    lax.fori_loop(0, tc, step, 0)

    @pl.when(c == pl.num_programs(1) - 1)
    def _():
        st_ref[0] = S[...]


def _rwkv_scan(a, wr, w, b, k, vf, br, kr, s0):
    B, S = a.shape[:2]
    tc = _pick(S, (128, 64, 32, 16, 8))
    grid = (B, S // tc)
    kspec = pl.BlockSpec((1, tc, N_KH, LANES), lambda b, c: (b, c, 0, 0))
    vspec = pl.BlockSpec((1, tc, HEAD_DIM, LANES), lambda b, c: (b, c, 0, 0))
    rspec = pl.BlockSpec((1, tc, 1, LANES), lambda b, c: (b, c, 0, 0))
    sspec = pl.BlockSpec((1, N_STATE_TILES, SUBLANES, LANES), lambda b, c: (b, 0, 0, 0))
    return pl.pallas_call(
        functools.partial(_rwkv_scan_kernel, tc=tc),
        out_shape=[jax.ShapeDtypeStruct((B, S, HEAD_DIM, LANES), F32),
                   jax.ShapeDtypeStruct((B, N_STATE_TILES, SUBLANES, LANES), F32)],
        grid=grid,
        in_specs=[kspec] * 5 + [vspec, rspec, rspec, sspec],
        out_specs=[vspec, sspec],
        scratch_shapes=[pltpu.VMEM((N_STATE_TILES, SUBLANES, LANES), F32)],
        compiler_params=_cparams(("parallel", "arbitrary")), name="rwkv_scan",
    )(a, wr, w, b, k, vf, br, kr, s0)


def _rwkv_post_kernel(y_ref, v_ref, g_ref, rkr_ref, gw_ref, gb_ref, o_ref):
    y = y_ref[0]
    inv_n = 1.0 / HEAD_DIM
    d = y - _tile16(_head_sum(y) * inv_n)
    var = _head_sum(d * d) * inv_n
    yn = d * _tile16(lax.rsqrt(var + GN_EPS)) * gw_ref[0] + gb_ref[0]
    o_ref[0] = ((yn + _tile16(rkr_ref[0]) * v_ref[0]) * g_ref[0]).astype(o_ref.dtype)


def _rwkv_post(y, v, g, rkr, lw):
    B, S, _ = y.shape
    ts = _pick(S, (256, 128, 64, 32, 16, 8))
    big = pl.BlockSpec((1, ts, MIX_WIDTH), lambda b, s: (b, s, 0))
    small = pl.BlockSpec((1, ts, LANES), lambda b, s: (b, s, 0))
    vec = pl.BlockSpec((1, 1, MIX_WIDTH), lambda b, s: (0, 0, 0))
    return pl.pallas_call(
        _rwkv_post_kernel, out_shape=jax.ShapeDtypeStruct((B, S, MIX_WIDTH), BF16),
        grid=(B, S // ts), in_specs=[big, big, big, small, vec, vec], out_specs=big,
        compiler_params=_cparams(("parallel", "parallel")), name="rwkv_post",
    )(y, v, g, rkr, lw["gn_w"], lw["gn_b"])


def _conv_kernel(bg_ref, cg_ref, xin_ref, pcg_ref, pxin_ref, buf_ref, cw_ref, y_ref, st_ref):
    s = pl.program_id(2)
    u = cg_ref[0] * xin_ref[0]
    ts = u.shape[0]
    pu = pcg_ref[0] * pxin_ref[0]
    prev2 = jnp.where(s == 0, buf_ref[0], pu[SUBLANES - 2:SUBLANES])
    rid = lax.broadcasted_iota(jnp.int32, u.shape, 0)
    u1 = jnp.where(rid == 0, prev2[1:2], pltpu.roll(u, 1, axis=0))
    u2 = jnp.where(rid == 0, prev2[0:1], jnp.where(rid == 1, prev2[1:2], pltpu.roll(u, 2, axis=0)))
    cw = cw_ref[...]
    y = cw[0:1] * u2 + cw[1:2] * u1 + cw[2:3] * u
    y_ref[0] = (bg_ref[0] * y).astype(y_ref.dtype)

    @pl.when(s == pl.num_programs(2) - 1)
    def _():
        st_ref[0] = u[ts - (CONV_WIDTH - 1):ts]


def _conv(z_all, buf, conv_w, layer):
    B, S, _ = z_all.shape
    ts = _pick(S, (512, 256, 128, 64, 32, 16, 8))
    nb8 = ts // SUBLANES
    nj = MIX_WIDTH // COL_BLK
    c0 = OFF_CONV // COL_BLK
    blk = lambda off: pl.BlockSpec((1, ts, COL_BLK), lambda b, j, s, off=off: (b, s, off + j))
    pblk = lambda off: pl.BlockSpec(
        (1, SUBLANES, COL_BLK), lambda b, j, s, off=off: (b, jnp.maximum(s * nb8 - 1, 0), off + j))
    st_spec = pl.BlockSpec((1, CONV_WIDTH - 1, COL_BLK), lambda b, j, s: (b, 0, j))
    return pl.pallas_call(
        _conv_kernel,
        out_shape=[jax.ShapeDtypeStruct((B, S, MIX_WIDTH), BF16),
                   jax.ShapeDtypeStruct((B, CONV_WIDTH - 1, MIX_WIDTH), F32)],
        grid=(B, nj, S // ts),
        in_specs=[blk(c0), blk(c0 + nj), blk(c0 + 2 * nj), pblk(c0 + nj), pblk(c0 + 2 * nj), st_spec,
                  pl.BlockSpec((None, CONV_WIDTH, COL_BLK), lambda b, j, s: (layer, 0, j))],
        out_specs=[pl.BlockSpec((1, ts, COL_BLK), lambda b, j, s: (b, s, j)), st_spec],
        compiler_params=_cparams(("parallel", "parallel", "arbitrary")), name="gated_conv",
    )(z_all, z_all, z_all, z_all, z_all, buf, conv_w)


def _attn_core(q_refs, k_all, v_all, sinks_ref, q0, k0, o_ref):
    tq = q_refs[0].shape[1]
    tk = k_all.shape[0]
    qpos = q0 + lax.broadcasted_iota(jnp.int32, (tq, tk), 0)
    kpos = k0 + lax.broadcasted_iota(jnp.int32, (tq, tk), 1)
    dc = (qpos >> CHUNK_SHIFT) - (kpos >> CHUNK_SHIFT)
    valid = (dc >= 0) & (dc <= W_CHUNKS) & (kpos >= 0)
    dist = jnp.abs(qpos - kpos).astype(F32)
    heads_per_blk = COL_BLK // HEAD_DIM
    for g in range(KV_HEADS):
        kg = k_all[:, g * HEAD_DIM:(g + 1) * HEAD_DIM].astype(BF16)
        vg = v_all[:, g * HEAD_DIM:(g + 1) * HEAD_DIM].astype(BF16)
        for j in range(GROUP):
            h = g * GROUP + j
            lo = (h % heads_per_blk) * HEAD_DIM
            qh = q_refs[h // heads_per_blk][0, :, lo:lo + HEAD_DIM].astype(BF16)
            s = lax.dot_general(qh, kg, (((1,), (1,)), ((), ())), preferred_element_type=F32)
            slope = 2.0 ** (-8.0 * (h + 1) / N_HEADS)
            s = jnp.where(valid, s * ATTN_SCALE - slope * dist, -1e30)
            sk = sinks_ref[h]
            m = jnp.maximum(jnp.max(s, axis=-1, keepdims=True), sk)
            p = jnp.exp(s - m)
            den = jnp.sum(p, axis=-1, keepdims=True) + jnp.exp(sk - m)
            o = jnp.dot(p.astype(BF16), vg, preferred_element_type=F32) / den
            o_ref[0, :, h * HEAD_DIM:(h + 1) * HEAD_DIM] = o.astype(o_ref.dtype)


def _attn_prompt_kernel(sinks_ref, q0_ref, q1_ref, q2_ref, q3_ref, ka_ref, kb_ref, kc_ref,
                        va_ref, vb_ref, vc_ref, o_ref):
    n = pl.program_id(1)
    k_all = jnp.concatenate([ka_ref[0], kb_ref[0], kc_ref[0]], axis=0)
    v_all = jnp.concatenate([va_ref[0], vb_ref[0], vc_ref[0]], axis=0)
    _attn_core((q0_ref, q1_ref, q2_ref, q3_ref), k_all, v_all, sinks_ref,
               n * CHUNK, (n - W_CHUNKS) * CHUNK, o_ref)


def _attn_prompt(z_all, sinks):
    B, S, _ = z_all.shape
    qb = OFF_ATTN // COL_BLK
    kb = qb + MIX_WIDTH // COL_BLK
    vb = kb + 1
    qspec = lambda i: pl.BlockSpec((1, CHUNK, COL_BLK), lambda b, n, s, i=i: (b, n, qb + i))
    wspec = lambda back, col: pl.BlockSpec(
        (1, CHUNK, COL_BLK), lambda b, n, s, back=back, col=col: (b, jnp.maximum(n - back, 0), col))
    grid_spec = pltpu.PrefetchScalarGridSpec(
        num_scalar_prefetch=1, grid=(B, S // CHUNK),
        in_specs=[qspec(0), qspec(1), qspec(2), qspec(3),
                  wspec(2, kb), wspec(1, kb), wspec(0, kb), wspec(2, vb), wspec(1, vb), wspec(0, vb)],
        out_specs=pl.BlockSpec((1, CHUNK, MIX_WIDTH), lambda b, n, s: (b, n, 0)))
    return pl.pallas_call(
        _attn_prompt_kernel, out_shape=jax.ShapeDtypeStruct((B, S, MIX_WIDTH), BF16),
        grid_spec=grid_spec, compiler_params=_cparams(("parallel", "parallel")), name="swa_prompt",
    )(sinks, *([z_all] * 10))


def _attn_sample_kernel(sinks_ref, q0_ref, q1_ref, q2_ref, q3_ref, kc_ref, kn_ref, vc_ref, vn_ref,
                        o_ref, *, pos0):
    L = kc_ref.shape[1]
    k_all = jnp.concatenate([kc_ref[0], kn_ref[0]], axis=0)
    v_all = jnp.concatenate([vc_ref[0], vn_ref[0]], axis=0)
    _attn_core((q0_ref, q1_ref, q2_ref, q3_ref), k_all, v_all, sinks_ref, pos0, pos0 - L, o_ref)


def _attn_sample(z_all, k_cache, v_cache, sinks, pos0):
    B, S, _ = z_all.shape
    L = k_cache.shape[1]
    qb = OFF_ATTN // COL_BLK
    kb = qb + MIX_WIDTH // COL_BLK
    vb = kb + 1
    qspec = lambda i: pl.BlockSpec((1, S, COL_BLK), lambda b, s, i=i: (b, 0, qb + i))
    nspec = lambda col: pl.BlockSpec((1, S, COL_BLK), lambda b, s, col=col: (b, 0, col))
    cspec = pl.BlockSpec((1, L, KV_WIDTH), lambda b, s: (b, 0, 0))
    grid_spec = pltpu.PrefetchScalarGridSpec(
        num_scalar_prefetch=1, grid=(B,),
        in_specs=[qspec(0), qspec(1), qspec(2), qspec(3), cspec, nspec(kb), cspec, nspec(vb)],
        out_specs=pl.BlockSpec((1, S, MIX_WIDTH), lambda b, s: (b, 0, 0)))
    return pl.pallas_call(
        functools.partial(_attn_sample_kernel, pos0=pos0),
        out_shape=jax.ShapeDtypeStruct((B, S, MIX_WIDTH), BF16),
        grid_spec=grid_spec, compiler_params=_cparams(("parallel",)), name="swa_sample",
    )(sinks, z_all, z_all, z_all, z_all, k_cache, z_all, v_cache, z_all)


def _merge_kernel(ya_ref, yb_ref, yc_ref, w0_ref, w12_ref, zg_ref, o_ref, acc_ref, *, tm):
    br = pl.program_id(3)
    gate = jax.nn.sigmoid(zg_ref[...].reshape(tm, zg_ref.shape[-1]))

    def contrib(y_ref, w_ref):
        y = y_ref[...].reshape(tm, y_ref.shape[-1])
        return gate * jnp.dot(y, w_ref[...].astype(BF16), preferred_element_type=F32)

    @pl.when(br == 0)
    def _():
        acc_ref[...] = contrib(ya_ref, w0_ref)

    @pl.when(br == 1)
    def _():
        acc_ref[...] += contrib(yb_ref, w12_ref)

    @pl.when(br == 2)
    def _():
        o_ref[...] = (acc_ref[...] + contrib(yc_ref, w12_ref)).reshape(o_ref.shape).astype(o_ref.dtype)


def _merge(ya, yb, yc, w0p, w_branch_out, layer, z_all, D):
    B, S, _ = ya.shape
    bb, ts = _row_tiles(B, S, 1024)
    tm = bb * ts
    tn = COL_BLK
    g0 = OFF_GATE // tn
    nd = D // tn
    yspec = pl.BlockSpec((bb, ts, MIX_WIDTH), lambda b, m, n, r: (b, m, 0))
    return pl.pallas_call(
        functools.partial(_merge_kernel, tm=tm),
        out_shape=jax.ShapeDtypeStruct((B, S, D), BF16),
        grid=(B // bb, S // ts, nd, N_BRANCH),
        in_specs=[yspec, yspec, yspec,
                  pl.BlockSpec((MIX_WIDTH, tn), lambda b, m, n, r: (0, n)),
                  pl.BlockSpec((None, None, MIX_WIDTH, tn),
                               lambda b, m, n, r: (layer, jnp.maximum(r, 1), 0, n)),
                  pl.BlockSpec((bb, ts, tn), lambda b, m, n, r: (b, m, g0 + r * nd + n))],
        out_specs=pl.BlockSpec((bb, ts, tn), lambda b, m, n, r: (b, m, n)),
        scratch_shapes=[pltpu.VMEM((tm, tn), F32)],
        compiler_params=_cparams(("parallel", "parallel", "parallel", "arbitrary")), name="branch_merge",
    )(ya, yb, yc, w0p, w_branch_out, z_all)


def _im_cols(w):
    lead = w.shape[:-1]
    return jnp.swapaxes(w.reshape(lead + (N_HEADS, HEAD_DIM)), -1, -2).reshape(lead + (MIX_WIDTH,))


def _im_cols_inv(w):
    lead = w.shape[:-1]
    return jnp.swapaxes(w.reshape(lead + (HEAD_DIM, N_HEADS)), -1, -2).reshape(lead + (MIX_WIDTH,))


def _pack_rwkv_cols(t):
    lead = t.shape[:-1]
    z32 = jnp.zeros(lead + (LANES - DECAY_LORA,), t.dtype)
    o = 3 * MIX_WIDTH
    return jnp.concatenate([
        _im_cols(t[..., 0:MIX_WIDTH]), _im_cols(t[..., MIX_WIDTH:2 * MIX_WIDTH]),
        _im_cols(t[..., 2 * MIX_WIDTH:3 * MIX_WIDTH]),
        t[..., o:o + DECAY_LORA], z32,
        t[..., o + DECAY_LORA:o + DECAY_LORA + ICLR_LORA], z32,
        t[..., o + DECAY_LORA + ICLR_LORA:]], axis=-1)


def _unpack_rwkv_cols(t):
    o = OFF_LORA
    return jnp.concatenate([
        _im_cols_inv(t[..., 0:MIX_WIDTH]), _im_cols_inv(t[..., MIX_WIDTH:2 * MIX_WIDTH]),
        _im_cols_inv(t[..., 2 * MIX_WIDTH:3 * MIX_WIDTH]),
        t[..., o:o + DECAY_LORA], t[..., o + LANES:o + LANES + ICLR_LORA],
        t[..., o + 2 * LANES:o + 2 * LANES + GATE_LORA]], axis=-1)


def _pad_rows(w, rows):
    return jnp.concatenate([w, jnp.zeros((rows - w.shape[0],) + w.shape[1:], w.dtype)], axis=0)


def _state_to_tiles(s):
    B = s.shape[0]
    s = s.reshape(B, N_HEADS, N_VH, SUBLANES, N_KH, 4)
    s = jnp.transpose(s, (0, 2, 4, 3, 5, 1))
    return s.reshape(B, N_STATE_TILES, SUBLANES, LANES)


def _tiles_to_state(s):
    B = s.shape[0]
    s = s.reshape(B, N_VH, N_KH, SUBLANES, 4, N_HEADS)
    s = jnp.transpose(s, (0, 5, 1, 3, 2, 4))
    return s.reshape(B, N_HEADS, HEAD_DIM, HEAD_DIM)


def _layer(i, x, mod, st, pos0, lw, P):
    B, S, D = x.shape
    m = [mod[:, j:j + 1] for j in range(6)]
    h = _norm(x, P["norm_mix_g"], i, m[1], m[0], name="norm_mix")
    z_all = _mm(h, [lw["w_all"]], [()], name="in_proj")

    if st is None:
        s0 = jnp.zeros((B, N_STATE_TILES, SUBLANES, LANES), F32)
        shift0 = jnp.zeros((B, 1, RWKV_PAD), F32)
        conv0 = jnp.zeros((B, CONV_WIDTH - 1, MIX_WIDTH), F32)
    else:
        s0 = _state_to_tiles(st[0].astype(F32))
        shift0 = _pack_rwkv_cols(st[1].astype(F32))
        conv0 = st[2].astype(F32)

    a, wr, w, b, k, v, g, br, kr, rkr = _rwkv_prep(z_all, shift0, lw)
    t4 = lambda t: t.reshape(B, S, N_KH, LANES)
    vf = jnp.tile(v.reshape(B, S, HEAD_DIM, N_HEADS), (1, 1, 1, LANES // N_HEADS))
    y4, s_tiles = _rwkv_scan(t4(a), t4(wr), t4(w), t4(b), t4(k), vf,
                             br.reshape(B, S, 1, LANES), kr.reshape(B, S, 1, LANES), s0)
    y = y4[..., :N_HEADS].reshape(B, S, MIX_WIDTH)
    y_a = _rwkv_post(y, v, g, rkr, lw)
    s_new = _tiles_to_state(s_tiles)
    shift_new = _unpack_rwkv_cols(z_all[:, S - 1:S, 0:RWKV_PAD])

    y_b, conv_new = _conv(z_all, conv0, P["conv_w"], i)

    k_new_all = z_all[:, :, OFF_ATTN + MIX_WIDTH:OFF_ATTN + MIX_WIDTH + KV_WIDTH]
    v_new_all = z_all[:, :, OFF_ATTN + MIX_WIDTH + KV_WIDTH:OFF_ATTN + MIX_WIDTH + 2 * KV_WIDTH]
    if st is None:
        y_c = _attn_prompt(z_all, lw["sinks"])
        k_new = k_new_all[:, S - WINDOW:].reshape(B, WINDOW, KV_HEADS, HEAD_DIM)
        v_new = v_new_all[:, S - WINDOW:].reshape(B, WINDOW, KV_HEADS, HEAD_DIM)
    else:
        L = st[3].shape[1]
        kc = st[3].astype(F32).reshape(B, L, KV_WIDTH)
        vc = st[4].astype(F32).reshape(B, L, KV_WIDTH)
        y_c = _attn_sample(z_all, kc, vc, lw["sinks"], pos0)
        k_new = jnp.concatenate([kc, k_new_all], axis=1)[:, -L:].reshape(B, L, KV_HEADS, HEAD_DIM)
        v_new = jnp.concatenate([vc, v_new_all], axis=1)[:, -L:].reshape(B, L, KV_HEADS, HEAD_DIM)

    merged = _merge(y_a, y_b, y_c, lw["wb0"], P["w_branch_out"], i, z_all, D)
    x = _mm(merged, [P["w_mix_out"]], [(i,)], epi="resgate", extras=(x, m[2]), name="mix_out")

    j = i // 2
    if i % 2 == 0:
        h = _norm(x, P["norm_ffn_g"], i, m[4], m[3], name="norm_ffn")
        act = _mm(h, [P["ffn_w_gate"], P["ffn_w_up"]], [(j,), (j,)], epi="swiglu", out_dtype=BF16,
                  name="ffn_up")
        x = _mm(act, [P["ffn_w_down"]], [(j,)], epi="resgate", extras=(x, m[5]), name="ffn_down")
    else:
        n_exp = P["moe_w_gate"].shape[1]
        h, logits = _norm(x, P["norm_ffn_g"], i, m[4], m[3], router=lw["router"], name="norm_moe")
        gate = _topk_gate(logits, n_exp)
        for e in range(n_exp):
            act = _mm(h, [P["moe_w_gate"], P["moe_w_up"]], [(j, e), (j, e)], epi="swiglu",
                      out_dtype=BF16, name="moe_up")
            x = _mm(act, [P["moe_w_down"]], [(j, e)], epi="resgate_rs", extras=(x, m[5], gate),
                    e_col=e, name="moe_down")
    return x, (s_new, shift_new, conv_new, k_new, v_new)


def kernel(x_prompt, x_sample, c_prompt, c_sample, state_rwkv, state_rwkv_shift, state_conv, cache_swa_k, cache_swa_v, ada_w, ada_b, norm_mix_g, norm_ffn_g, w_in, rwkv_mu, rwkv_w_up, rwkv_w0, rwkv_a_up, rwkv_a0, rwkv_g_up, rwkv_k_k, rwkv_k_a, rwkv_r_k, rwkv_gn_w, rwkv_gn_b, conv_w, attn_sinks, w_branch_out, w_mix_out, ffn_w_gate, ffn_w_up, ffn_w_down, moe_router, moe_w_gate, moe_w_up, moe_w_down, final_norm_g):
    depth, D = norm_mix_g.shape
    Bp, Sp, _ = x_prompt.shape
    Bs, Ss, _ = x_sample.shape

    P = dict(norm_mix_g=norm_mix_g.reshape(depth, 1, D), norm_ffn_g=norm_ffn_g.reshape(depth, 1, D),
             conv_w=conv_w, w_branch_out=w_branch_out, w_mix_out=w_mix_out,
             ffn_w_gate=ffn_w_gate, ffn_w_up=ffn_w_up, ffn_w_down=ffn_w_down,
             moe_w_gate=moe_w_gate, moe_w_up=moe_w_up, moe_w_down=moe_w_down)

    nc = Bp + Bs
    nc_pad = -(-nc // 16) * 16
    c_all = jnp.concatenate([c_prompt, c_sample, jnp.zeros((nc_pad - nc, D), F32)], axis=0)[None]
    ada_b3 = ada_b.reshape(depth, 1, 6 * D)

    xp, xs = x_prompt, x_sample
    outs_p, outs_s = [], []
    for i in range(depth):
        mod = _mm(c_all, [ada_w], [(i,)], epi="bias", extras=(ada_b3, i), silu_in=True,
                  name="ada_mod")[0]
        mod_p = mod[:Bp].reshape(Bp, 6, D)
        mod_s = mod[Bp:nc].reshape(Bs, 6, D)

        o = 3 * MIX_WIDTH
        wi = w_in[i]
        zc = jnp.zeros((D, LANES - DECAY_LORA), F32)
        w_all = jnp.concatenate([
            _im_cols(wi[:, 0:MIX_WIDTH]), _im_cols(wi[:, MIX_WIDTH:2 * MIX_WIDTH]),
            _im_cols(wi[:, 2 * MIX_WIDTH:3 * MIX_WIDTH]),
            wi[:, o:o + DECAY_LORA], zc, wi[:, o + DECAY_LORA:o + DECAY_LORA + ICLR_LORA], zc,
            wi[:, o + DECAY_LORA + ICLR_LORA:]], axis=1)
        vec = lambda t: _im_cols(t[i]).reshape(1, 1, MIX_WIDTH)
        lw = dict(
            w_all=w_all,
            mu=_pack_rwkv_cols(rwkv_mu[i]).reshape(1, 1, RWKV_PAD),
            w_up=_pad_rows(_im_cols(rwkv_w_up[i]), LANES)[None], w0=vec(rwkv_w0),
            a_up=_pad_rows(_im_cols(rwkv_a_up[i]), LANES)[None], a0=vec(rwkv_a0),
            g_up=_im_cols(rwkv_g_up[i])[None],
            k_k=vec(rwkv_k_k), k_a=vec(rwkv_k_a), r_k=vec(rwkv_r_k),
            gn_w=vec(rwkv_gn_w), gn_b=vec(rwkv_gn_b),
            sinks=attn_sinks[i],
            wb0=jnp.swapaxes(w_branch_out[i, 0].reshape(N_HEADS, HEAD_DIM, D), 0, 1).reshape(MIX_WIDTH, D),
        )
        if i % 2 == 1:
            r = moe_router[i // 2]
            lw["router"] = jnp.concatenate([r, jnp.zeros((D, LANES - r.shape[1]), F32)], axis=1)

        xp, st_p = _layer(i, xp, mod_p, None, 0, lw, P)
        st_s_in = (state_rwkv[i], state_rwkv_shift[i], state_conv[i], cache_swa_k[i], cache_swa_v[i])
        xs, st_s = _layer(i, xs, mod_s, st_s_in, PAST_LEN, lw, P)
        outs_p.append(st_p)
        outs_s.append(st_s)

    fg = final_norm_g.reshape(1, 1, D)
    y_prompt = _norm(xp, fg, 0, out_dtype=F32, name="norm_final")
    y_sample = _norm(xs, fg, 0, out_dtype=F32, name="norm_final")
    stack = lambda outs, idx: jnp.stack([o[idx] for o in outs])
    return (y_prompt, y_sample,
            stack(outs_p, 0), stack(outs_p, 1), stack(outs_p, 2), stack(outs_p, 3), stack(outs_p, 4),
            stack(outs_s, 0), stack(outs_s, 1), stack(outs_s, 2), stack(outs_s, 3), stack(outs_s, 4))
```

```python
import functools
import math

import jax
import jax.numpy as jnp
from jax import lax
from jax.experimental import pallas as pl
from jax.experimental.pallas import tpu as pltpu

F32 = jnp.float32
BF16 = jnp.bfloat16

HEAD_DIM = 64
N_HEADS = 32
MIX_WIDTH = N_HEADS * HEAD_DIM
KV_HEADS = 8
GROUP = N_HEADS // KV_HEADS
KV_WIDTH = KV_HEADS * HEAD_DIM
CHUNK = 64
CHUNK_SHIFT = 6
W_CHUNKS = 2
WINDOW = W_CHUNKS * CHUNK
PAST_LEN = 1024
CONV_WIDTH = 3
DECAY_LORA = 96
ICLR_LORA = 96
GATE_LORA = 256
N_BRANCH = 3
TOP_K = 2
RMS_EPS = 1e-5
GN_EPS = 64e-5
ATTN_SCALE = HEAD_DIM ** -0.5

LANES = 128
SUBLANES = 8
VMEM_LIMIT_BYTES = 56 * 1024 * 1024

LORA_PAD = 512
OFF_R, OFF_K, OFF_V = 0, MIX_WIDTH, 2 * MIX_WIDTH
OFF_LORA = 3 * MIX_WIDTH
RWKV_PAD = OFF_LORA + LORA_PAD
OFF_CONV = RWKV_PAD
OFF_ATTN = OFF_CONV + 3 * MIX_WIDTH
OFF_GATE = OFF_ATTN + MIX_WIDTH + 2 * KV_WIDTH
COL_BLK = 512
RWKV_COLS = 3 * MIX_WIDTH + DECAY_LORA + ICLR_LORA + GATE_LORA


def _cparams(sem):
    return pltpu.CompilerParams(dimension_semantics=sem, vmem_limit_bytes=VMEM_LIMIT_BYTES)


def _pick(n, cands):
    for c in cands:
        if n % c == 0:
            return c
    return n


def _row_tiles(B, S, tm_max):
    if S >= tm_max:
        ts = _pick(S, (tm_max, tm_max // 2, tm_max // 4, 256, 128, 64, 32, 16, 8))
        return 1, ts
    bb = max(1, min(B, tm_max // S))
    while B % bb:
        bb -= 1
    return bb, S


def _mm_kernel(*refs, n_w, epi, tm, nk, silu_in):
    x_ref = refs[0]
    w_refs = refs[1:1 + n_w]
    pos = 1 + n_w
    n_extra = {"none": 0, "swiglu": 0, "bias": 1, "resgate": 2}[epi]
    extra = refs[pos:pos + n_extra]
    pos += n_extra
    o_ref = refs[pos]
    acc_refs = refs[pos + 1:]

    x = x_ref[...]
    x = x.reshape(tm, x.shape[-1])
    if silu_in:
        xf = x.astype(F32)
        x = xf * jax.nn.sigmoid(xf)
    x = x.astype(BF16)

    def finish(accs):
        a = accs[0]
        if epi == "swiglu":
            r = (a * jax.nn.sigmoid(a)) * accs[1]
        elif epi == "bias":
            r = a + extra[0][...]
        elif epi == "resgate":
            r = extra[0][...] + extra[1][...] * a.reshape(o_ref.shape)
        else:
            r = a
        o_ref[...] = r.reshape(o_ref.shape).astype(o_ref.dtype)

    dots = [jnp.dot(x, w_refs[j][...].astype(BF16), preferred_element_type=F32) for j in range(n_w)]
    if nk == 1:
        finish(dots)
        return

    k = pl.program_id(3)
    if not acc_refs:
        acc_refs = (o_ref,)

    @pl.when(k == 0)
    def _():
        for acc in acc_refs:
            acc[...] = jnp.zeros_like(acc)

    for j in range(n_w):
        acc_refs[j][...] += dots[j].reshape(acc_refs[j].shape)

    if epi != "none" or acc_refs[0] is not o_ref:
        @pl.when(k == nk - 1)
        def _():
            finish([acc[...] for acc in acc_refs])


def _mm(x, ws, leads, *, tm_max, tn, tk, epi="none", out_dtype=F32, extras=(), silu_in=False,
        name="mm"):
    B, S, K = x.shape
    N = ws[0].shape[-1]
    bb, ts = _row_tiles(B, S, tm_max)
    tm = bb * ts
    tn = _pick(N, (tn, 512, 256, 128))
    tk = _pick(K, (tk, 1024, 512, 256, 128))
    nk = K // tk
    grid = (B // bb, S // ts, N // tn, nk)

    in_specs = [pl.BlockSpec((bb, ts, tk), lambda b, m, n, k: (b, m, k))]
    for w, lead in zip(ws, leads):
        nl = len(lead)
        in_specs.append(pl.BlockSpec((None,) * nl + (tk, tn),
                                     lambda b, m, n, k, lead=lead: tuple(lead) + (k, n)))
    if epi == "bias":
        in_specs.append(pl.BlockSpec((None, 1, tn), lambda b, m, n, k, l=extras[1]: (l, 0, n)))
        extras = extras[:1]
    elif epi == "resgate":
        in_specs.append(pl.BlockSpec((bb, ts, tn), lambda b, m, n, k: (b, m, n)))
        in_specs.append(pl.BlockSpec((bb, 1, tn), lambda b, m, n, k: (b, 0, n)))
    n_w = len(ws)
    direct = nk == 1 or (n_w == 1 and out_dtype == F32)
    kern = functools.partial(_mm_kernel, n_w=n_w, epi=epi, tm=tm, nk=nk, silu_in=silu_in)
    return pl.pallas_call(
        kern,
        out_shape=jax.ShapeDtypeStruct((B, S, N), out_dtype),
        grid=grid,
        in_specs=in_specs,
        out_specs=pl.BlockSpec((bb, ts, tn), lambda b, m, n, k: (b, m, n)),
        scratch_shapes=[] if direct else [pltpu.VMEM((tm, tn), F32) for _ in range(n_w)],
        compiler_params=_cparams(("parallel", "parallel", "parallel", "arbitrary")),
        name=name,
    )(x, *ws, *extras)


def _norm_kernel(*refs, modulate, route):
    x_ref, g_ref = refs[0], refs[1]
    pos = 2
    if modulate:
        sc_ref, sh_ref = refs[2], refs[3]
        pos = 4
    if route:
        r_ref = refs[pos]
        pos += 1
    o_ref = refs[pos]
    x = x_ref[...]
    ms = jnp.mean(x * x, axis=-1, keepdims=True)
    y = x * lax.rsqrt(ms + RMS_EPS) * g_ref[...]
    if modulate:
        y = y * (1.0 + sc_ref[...]) + sh_ref[...]
    o_ref[...] = y.astype(o_ref.dtype)
    if route:
        l_ref = refs[pos + 1]
        y2 = y.reshape(-1, y.shape[-1])
        logits = jnp.dot(y2, r_ref[...], preferred_element_type=F32, precision=lax.Precision.HIGHEST)
        l_ref[...] = logits.reshape(l_ref.shape)


def _norm(x, g3, lead, scale=None, shift=None, router=None, out_dtype=BF16, name="norm"):
    B, S, D = x.shape
    bb, ts = _row_tiles(B, S, 256)
    grid = (B // bb, S // ts)
    modulate = scale is not None
    route = router is not None
    in_specs = [pl.BlockSpec((bb, ts, D), lambda b, m: (b, m, 0)),
                pl.BlockSpec((None, 1, D), lambda b, m: (lead, 0, 0))]
    args = [x, g3]
    if modulate:
        in_specs += [pl.BlockSpec((bb, 1, D), lambda b, m: (b, 0, 0))] * 2
        args += [scale, shift]
    out_shape = [jax.ShapeDtypeStruct((B, S, D), out_dtype)]
    out_specs = [pl.BlockSpec((bb, ts, D), lambda b, m: (b, m, 0))]
    if route:
        in_specs.append(pl.BlockSpec((D, LANES), lambda b, m: (0, 0)))
        args.append(router)
        out_shape.append(jax.ShapeDtypeStruct((B, S, LANES), F32))
        out_specs.append(pl.BlockSpec((bb, ts, LANES), lambda b, m: (b, m, 0)))
    res = pl.pallas_call(
        functools.partial(_norm_kernel, modulate=modulate, route=route),
        out_shape=out_shape, grid=grid, in_specs=in_specs, out_specs=out_specs,
        compiler_params=_cparams(("parallel", "parallel")), name=name,
    )(*args)
    return res if route else res[0]


def _topk_kernel(l_ref, w_ref, i_ref, *, n_exp):
    l = l_ref[...]
    lane = lax.broadcasted_iota(jnp.int32, l.shape, l.ndim - 1)
    neg = jnp.float32(-jnp.inf)
    l = jnp.where(lane < n_exp, l, neg)
    m1 = jnp.max(l, axis=-1, keepdims=True)
    i1 = jnp.min(jnp.where(l == m1, lane, LANES), axis=-1, keepdims=True)
    l2 = jnp.where(lane == i1, neg, l)
    m2 = jnp.max(l2, axis=-1, keepdims=True)
    i2 = jnp.min(jnp.where(l2 == m2, lane, LANES), axis=-1, keepdims=True)
    e = jnp.exp(m2 - m1)
    den = 1.0 + e
    w_ref[...] = jnp.where(lane == 0, 1.0 / den, jnp.where(lane == 1, e / den, 0.0))
    i_ref[...] = jnp.where(lane == 0, i1, jnp.where(lane == 1, i2, 0))


def _topk_route(logits, n_exp):
    B, S, _ = logits.shape
    bb, ts = _row_tiles(B, S, 1024)
    spec = pl.BlockSpec((bb, ts, LANES), lambda b, m: (b, m, 0))
    return pl.pallas_call(
        functools.partial(_topk_kernel, n_exp=n_exp),
        out_shape=[jax.ShapeDtypeStruct(logits.shape, F32), jax.ShapeDtypeStruct(logits.shape, jnp.int32)],
        grid=(B // bb, S // ts), in_specs=[spec], out_specs=[spec, spec],
        compiler_params=_cparams(("parallel", "parallel")), name="topk_route",
    )(logits)


MOE_TILE = 1024
GATHER_ROWS = 256


def _row_copy(src_hbm, dst_vmem, sem, src_row, dst_row):
    return pltpu.make_async_copy(src_hbm.at[pl.ds(src_row, 1), :], dst_vmem.at[pl.ds(dst_row, 1), :], sem)


def _gather_kernel(tok_ref, h_hbm, o_ref, buf, sem, *, rows):
    base = pl.program_id(0) * rows

    def start(r, c):
        _row_copy(h_hbm, buf, sem, tok_ref[base + r], r).start()
        return c

    def wait(r, c):
        _row_copy(h_hbm, buf, sem, 0, r).wait()
        return c

    lax.fori_loop(0, rows, start, 0)
    lax.fori_loop(0, rows, wait, 0)
    o_ref[...] = buf[...].astype(o_ref.dtype)


def _gather_rows(h, row_token, n_rows):
    T, D = h.shape
    rows = GATHER_ROWS
    grid_spec = pltpu.PrefetchScalarGridSpec(
        num_scalar_prefetch=1, grid=(n_rows // rows,),
        in_specs=[pl.BlockSpec(memory_space=pl.ANY)],
        out_specs=pl.BlockSpec((rows, D), lambda i, tok: (i, 0)),
        scratch_shapes=[pltpu.VMEM((rows, D), F32), pltpu.SemaphoreType.DMA])
    return pl.pallas_call(
        functools.partial(_gather_kernel, rows=rows),
        out_shape=jax.ShapeDtypeStruct((n_rows, D), BF16), grid_spec=grid_spec,
        compiler_params=_cparams(("arbitrary",)), name="moe_gather",
    )(row_token, h)


def _gup_kernel(te_ref, nu_ref, x_ref, wg_ref, wu_ref, o_ref):
    m = pl.program_id(0)

    @pl.when(m < nu_ref[0])
    def _():
        x = x_ref[...]
        g = jnp.dot(x, wg_ref[...].astype(BF16), preferred_element_type=F32)
        u = jnp.dot(x, wu_ref[...].astype(BF16), preferred_element_type=F32)
        o_ref[...] = ((g * jax.nn.sigmoid(g)) * u).astype(o_ref.dtype)

    @pl.when(m >= nu_ref[0])
    def _():
        o_ref[...] = jnp.zeros_like(o_ref)


def _grouped_up(xg, wg, wu, j, tile_expert, n_used):
    P, D = xg.shape
    F = wg.shape[-1]
    tm = MOE_TILE
    tn = _pick(F, (256, 128))
    nn = F // tn

    def wmap(m, n, te, nu):
        return (j, te[m], 0, jnp.where(m < nu[0], n, nn - 1))

    grid_spec = pltpu.PrefetchScalarGridSpec(
        num_scalar_prefetch=2, grid=(P // tm, nn),
        in_specs=[pl.BlockSpec((tm, D), lambda m, n, te, nu: (jnp.minimum(m, nu[0] - 1), 0)),
                  pl.BlockSpec((None, None, D, tn), wmap), pl.BlockSpec((None, None, D, tn), wmap)],
        out_specs=pl.BlockSpec((tm, tn), lambda m, n, te, nu: (m, n)))
    return pl.pallas_call(
        _gup_kernel, out_shape=jax.ShapeDtypeStruct((P, F), BF16), grid_spec=grid_spec,
        compiler_params=_cparams(("arbitrary", "arbitrary")), name="moe_up",
    )(tile_expert, n_used, xg, wg, wu)


def _gdown_kernel(te_ref, nu_ref, x_ref, w_ref, o_ref):
    m = pl.program_id(0)
    k = pl.program_id(2)

    @pl.when(k == 0)
    def _():
        o_ref[...] = jnp.zeros_like(o_ref)

    @pl.when(m < nu_ref[0])
    def _():
        o_ref[...] += jnp.dot(x_ref[...], w_ref[...].astype(BF16), preferred_element_type=F32)


def _grouped_down(act, wd, j, tile_expert, n_used):
    P, F = act.shape
    D = wd.shape[-1]
    tm = MOE_TILE
    tn = _pick(D, (2048, 1024, 512, 256, 128))
    tk = _pick(F, (1024, 512, 256, 128))
    nn, nk = D // tn, F // tk

    def live(m, nu, idx, last):
        return jnp.where(m < nu[0], idx, last)

    grid_spec = pltpu.PrefetchScalarGridSpec(
        num_scalar_prefetch=2, grid=(P // tm, nn, nk),
        in_specs=[pl.BlockSpec((tm, tk), lambda m, n, k, te, nu: (jnp.minimum(m, nu[0] - 1), live(m, nu, k, nk - 1))),
                  pl.BlockSpec((None, None, tk, tn),
                               lambda m, n, k, te, nu: (j, te[m], live(m, nu, k, nk - 1), live(m, nu, n, nn - 1)))],
        out_specs=pl.BlockSpec((tm, tn), lambda m, n, k, te, nu: (m, n)))
    return pl.pallas_call(
        _gdown_kernel, out_shape=jax.ShapeDtypeStruct((P, D), F32), grid_spec=grid_spec,
        compiler_params=_cparams(("arbitrary", "arbitrary", "arbitrary")), name="moe_down",
    )(tile_expert, n_used, act, wd)


def _combine_kernel(pos_ref, og_hbm, w_ref, x_ref, g_ref, o_ref, buf, sem, *, rows, base):
    t0 = base + (pl.program_id(0) * pl.num_programs(1) + pl.program_id(1)) * rows

    def start(r, c):
        for jj in range(TOP_K):
            _row_copy(og_hbm, buf.at[jj], sem, pos_ref[TOP_K * (t0 + r) + jj], r).start()
        return c

    def wait(r, c):
        for jj in range(TOP_K):
            _row_copy(og_hbm, buf.at[jj], sem, 0, r).wait()
        return c

    lax.fori_loop(0, rows, start, 0)
    lax.fori_loop(0, rows, wait, 0)
    w = w_ref[0]
    f = w[:, 0:1] * buf[0] + w[:, 1:2] * buf[1]
    o_ref[0] = x_ref[0] + g_ref[0] * f


def _combine(og, pos, wsel, x, gate_f, base):
    B, S, D = x.shape
    rows = _pick(S, (256, 128, 64, 32, 16, 8))
    grid_spec = pltpu.PrefetchScalarGridSpec(
        num_scalar_prefetch=1, grid=(B, S // rows),
        in_specs=[pl.BlockSpec(memory_space=pl.ANY),
                  pl.BlockSpec((1, rows, LANES), lambda b, s, p: (b, s, 0)),
                  pl.BlockSpec((1, rows, D), lambda b, s, p: (b, s, 0)),
                  pl.BlockSpec((1, 1, D), lambda b, s, p: (b, 0, 0))],
        out_specs=pl.BlockSpec((1, rows, D), lambda b, s, p: (b, s, 0)),
        scratch_shapes=[pltpu.VMEM((TOP_K, rows, D), F32), pltpu.SemaphoreType.DMA])
    return pl.pallas_call(
        functools.partial(_combine_kernel, rows=rows, base=base),
        out_shape=jax.ShapeDtypeStruct((B, S, D), F32), grid_spec=grid_spec,
        compiler_params=_cparams(("arbitrary", "arbitrary")), name="moe_combine",
    )(pos, og, wsel, x, gate_f)


def _moe(hs, wsels, isels, xs, gates_f, wg, wu, wd, j):
    D = hs[0].shape[-1]
    n_exp = wg.shape[1]
    h_all = jnp.concatenate([h.reshape(-1, D) for h in hs], axis=0)
    e_flat = jnp.concatenate([i.reshape(-1, LANES)[:, :TOP_K] for i in isels], axis=0).reshape(-1)
    n_assign = e_flat.shape[0]
    n_tiles = -(-n_assign // MOE_TILE) + n_exp
    n_rows = n_tiles * MOE_TILE

    onehot = (e_flat[:, None] == jnp.arange(n_exp, dtype=jnp.int32)[None, :]).astype(jnp.int32)
    csum = jnp.cumsum(onehot, axis=0)
    rank = jnp.sum(csum * onehot, axis=1) - 1
    counts = csum[-1]
    gsize = ((counts + MOE_TILE - 1) // MOE_TILE) * MOE_TILE
    gend = jnp.cumsum(gsize)
    gstart = gend - gsize
    pos = (gstart[e_flat] + rank).astype(jnp.int32)
    row_token = jnp.zeros((n_rows,), jnp.int32).at[pos].set(jnp.arange(n_assign, dtype=jnp.int32) // TOP_K)
    n_used = (gend[-1] // MOE_TILE).astype(jnp.int32)
    tile_start = jnp.arange(n_tiles, dtype=jnp.int32) * MOE_TILE
    tile_expert = jnp.sum((tile_start[:, None] >= gend[None, :]).astype(jnp.int32), axis=1)
    last_expert = jnp.sum((((n_used - 1) * MOE_TILE) >= gend).astype(jnp.int32))
    tile_expert = jnp.where(tile_start < gend[-1], tile_expert, last_expert).astype(jnp.int32)
    n_used1 = n_used.reshape(1)

    xg = _gather_rows(h_all, row_token, n_rows)
    act = _grouped_up(xg, wg, wu, j, tile_expert, n_used1)
    og = _grouped_down(act, wd, j, tile_expert, n_used1)
    outs, base = [], 0
    for x, ws, gf in zip(xs, wsels, gates_f):
        outs.append(_combine(og, pos, ws, x, gf, base))
        base += x.shape[0] * x.shape[1]
    return outs


def _head_sum(x):
    s = x[:, 0:LANES]
    for i in range(1, MIX_WIDTH // LANES):
        s = s + x[:, i * LANES:(i + 1) * LANES]
    s = s + pltpu.roll(s, 64, axis=1)
    s = s + pltpu.roll(s, 32, axis=1)
    return s


def _tile16(s):
    return jnp.concatenate([s] * (MIX_WIDTH // LANES), axis=1)


def _rwkv_prep_kernel(z_ref, pz_ref, sh0_ref, mu_ref, wup_ref, w0_ref, aup_ref, a0_ref, gup_ref,
                      kk_ref, ka_ref, rk_ref,
                      a_out, wr_out, w_out, b_out, k_out, v_out, g_out, br_out, kr_out, rkr_out):
    s = pl.program_id(1)
    z = z_ref[0]
    ts = z.shape[0]
    prev = jnp.where(s == 0, sh0_ref[0], pz_ref[0][SUBLANES - 1:SUBLANES])
    rid = lax.broadcasted_iota(jnp.int32, z.shape, 0)
    zs = jnp.where(rid == 0, prev, pltpu.roll(z, 1, axis=0))
    zz = z + (zs - z) * mu_ref[0]
    r = zz[:, OFF_R:OFF_R + MIX_WIDTH]
    k = zz[:, OFF_K:OFF_K + MIX_WIDTH]
    v = zz[:, OFF_V:OFF_V + MIX_WIDTH]
    wl = jnp.tanh(zz[:, OFF_LORA:OFF_LORA + LANES])
    al = zz[:, OFF_LORA + LANES:OFF_LORA + 2 * LANES]
    gl = jax.nn.sigmoid(zz[:, OFF_LORA + 2 * LANES:OFF_LORA + 2 * LANES + GATE_LORA])

    w_raw = w0_ref[0] + jnp.dot(wl.astype(BF16), wup_ref[0].astype(BF16), preferred_element_type=F32)
    nx = -w_raw
    softplus = jnp.maximum(nx, 0.0) + jnp.log(1.0 + jnp.exp(-jnp.abs(nx)))
    w = jnp.exp(-jnp.exp(-softplus - 0.5))
    a = jax.nn.sigmoid(a0_ref[0] + jnp.dot(al.astype(BF16), aup_ref[0].astype(BF16),
                                           preferred_element_type=F32))
    g = jnp.dot(gl.astype(BF16), gup_ref[0].astype(BF16), preferred_element_type=F32)

    kk = k * kk_ref[0]
    nrm = jnp.sqrt(_head_sum(kk * kk))
    kk = kk * _tile16(1.0 / jnp.maximum(nrm, 1e-12))
    kt = k * (1.0 + (a - 1.0) * ka_ref[0])
    bs = kk * a

    a_out[0] = -kk
    wr_out[0] = w * r
    w_out[0] = w
    b_out[0] = bs
    k_out[0] = kt
    v_out[0] = v
    g_out[0] = g
    br_out[0] = _head_sum(bs * r)
    kr_out[0] = _head_sum(kt * r)
    rkr_out[0] = _head_sum(r * kt * rk_ref[0])


def _rwkv_prep(z_all, shift0, lw):
    B, S, NP = z_all.shape
    ts = _pick(S, (128, 64, 32, 16, 8))
    grid = (B, S // ts)
    nb8 = ts // SUBLANES
    vec = lambda n: pl.BlockSpec((1, 1, n), lambda b, s: (0, 0, 0))
    mat = lambda r, c: pl.BlockSpec((1, r, c), lambda b, s: (0, 0, 0))
    in_specs = [
        pl.BlockSpec((1, ts, RWKV_PAD), lambda b, s: (b, s, 0)),
        pl.BlockSpec((1, SUBLANES, RWKV_PAD), lambda b, s: (b, jnp.maximum(s * nb8 - 1, 0), 0)),
        pl.BlockSpec((1, 1, RWKV_PAD), lambda b, s: (b, 0, 0)),
        vec(RWKV_PAD),
        mat(LANES, MIX_WIDTH), vec(MIX_WIDTH),
        mat(LANES, MIX_WIDTH), vec(MIX_WIDTH),
        mat(GATE_LORA, MIX_WIDTH),
        vec(MIX_WIDTH), vec(MIX_WIDTH), vec(MIX_WIDTH),
    ]
    big = pl.BlockSpec((1, ts, MIX_WIDTH), lambda b, s: (b, s, 0))
    small = pl.BlockSpec((1, ts, LANES), lambda b, s: (b, s, 0))
    out_shape = [jax.ShapeDtypeStruct((B, S, MIX_WIDTH), F32)] * 7 + \
                [jax.ShapeDtypeStruct((B, S, LANES), F32)] * 3
    return pl.pallas_call(
        _rwkv_prep_kernel, out_shape=out_shape, grid=grid, in_specs=in_specs,
        out_specs=[big] * 7 + [small] * 3,
        compiler_params=_cparams(("parallel", "arbitrary")), name="rwkv_prep",
    )(z_all, z_all, shift0, lw["mu"], lw["w_up"], lw["w0"], lw["a_up"], lw["a0"], lw["g_up"],
      lw["k_k"], lw["k_a"], lw["r_k"])


N_VH = HEAD_DIM // SUBLANES
N_KH = MIX_WIDTH // LANES
N_STATE_TILES = N_VH * N_KH


def _lane_group_sum(x):
    x = x + pltpu.roll(x, 64, axis=1)
    return x + pltpu.roll(x, 32, axis=1)


def _rwkv_scan_kernel(a_ref, wr_ref, w_ref, b_ref, k_ref, v_ref, br_ref, kr_ref, s0_ref,
                      y_ref, st_ref, S, *, tc):
    c = pl.program_id(1)

    @pl.when(c == 0)
    def _():
        S[...] = s0_ref[0]

    def token(t, row):
        brr = row(br_ref, 0)
        krr = row(kr_ref, 0)
        for vh in range(N_VH):
            hs = [S[vh * N_KH + kh] for kh in range(N_KH)]
            au = hs[0] * row(a_ref, 0)
            ao = hs[0] * row(wr_ref, 0)
            for kh in range(1, N_KH):
                au = au + hs[kh] * row(a_ref, kh)
                ao = ao + hs[kh] * row(wr_ref, kh)
            u = _lane_group_sum(au)
            o = _lane_group_sum(ao)
            vv = v_ref[0, t, vh * SUBLANES:(vh + 1) * SUBLANES, :]
            y_ref[0, t, vh * SUBLANES:(vh + 1) * SUBLANES, :] = o + u * brr + vv * krr
            for kh in range(N_KH):
                S[vh * N_KH + kh] = hs[kh] * row(w_ref, kh) + row(b_ref, kh) * u + row(k_ref, kh) * vv

    def group(gi, carry):
        base = pl.multiple_of(gi * SUBLANES, SUBLANES)
        for jj in range(SUBLANES):
            def row(ref, kh, jj=jj):
                return ref[0, pl.ds(base, SUBLANES), kh * LANES:(kh + 1) * LANES][jj:jj + 1]
            token(base + jj, row)
        return carry

    lax.fori_loop(0, tc // SUBLANES, group, 0)

    @pl.when(c == pl.num_programs(1) - 1)
    def _():
        st_ref[0] = S[...]


def _rwkv_scan(a, wr, w, b, k, vf, br, kr, s0):
    B, S = a.shape[:2]
    tc = _pick(S, (128, 64, 32, 16, 8))
    grid = (B, S // tc)
    kspec = pl.BlockSpec((1, tc, MIX_WIDTH), lambda b, c: (b, c, 0))
    vspec = pl.BlockSpec((1, tc, HEAD_DIM, LANES), lambda b, c: (b, c, 0, 0))
    rspec = pl.BlockSpec((1, tc, LANES), lambda b, c: (b, c, 0))
    sspec = pl.BlockSpec((1, N_STATE_TILES, SUBLANES, LANES), lambda b, c: (b, 0, 0, 0))
    return pl.pallas_call(
        functools.partial(_rwkv_scan_kernel, tc=tc),
        out_shape=[jax.ShapeDtypeStruct((B, S, HEAD_DIM, LANES), F32),
                   jax.ShapeDtypeStruct((B, N_STATE_TILES, SUBLANES, LANES), F32)],
        grid=grid,
        in_specs=[kspec] * 5 + [vspec, rspec, rspec, sspec],
        out_specs=[vspec, sspec],
        scratch_shapes=[pltpu.VMEM((N_STATE_TILES, SUBLANES, LANES), F32)],
        compiler_params=_cparams(("parallel", "arbitrary")), name="rwkv_scan",
    )(a, wr, w, b, k, vf, br, kr, s0)


def _rwkv_post_kernel(y_ref, v_ref, g_ref, rkr_ref, gw_ref, gb_ref, o_ref):
    y = y_ref[0]
    inv_n = 1.0 / HEAD_DIM
    d = y - _tile16(_head_sum(y) * inv_n)
    var = _head_sum(d * d) * inv_n
    yn = d * _tile16(lax.rsqrt(var + GN_EPS)) * gw_ref[0] + gb_ref[0]
    o_ref[0] = ((yn + _tile16(rkr_ref[0]) * v_ref[0]) * g_ref[0]).astype(o_ref.dtype)


def _rwkv_post(y, v, g, rkr, lw):
    B, S, _ = y.shape
    ts = _pick(S, (256, 128, 64, 32, 16, 8))
    big = pl.BlockSpec((1, ts, MIX_WIDTH), lambda b, s: (b, s, 0))
    small = pl.BlockSpec((1, ts, LANES), lambda b, s: (b, s, 0))
    vec = pl.BlockSpec((1, 1, MIX_WIDTH), lambda b, s: (0, 0, 0))
    return pl.pallas_call(
        _rwkv_post_kernel, out_shape=jax.ShapeDtypeStruct((B, S, MIX_WIDTH), BF16),
        grid=(B, S // ts), in_specs=[big, big, big, small, vec, vec], out_specs=big,
        compiler_params=_cparams(("parallel", "parallel")), name="rwkv_post",
    )(y, v, g, rkr, lw["gn_w"], lw["gn_b"])


def _conv_kernel(bg_ref, cg_ref, xin_ref, pcg_ref, pxin_ref, buf_ref, cw_ref, y_ref, st_ref):
    s = pl.program_id(2)
    u = cg_ref[0] * xin_ref[0]
    ts = u.shape[0]
    pu = pcg_ref[0] * pxin_ref[0]
    prev2 = jnp.where(s == 0, buf_ref[0], pu[SUBLANES - 2:SUBLANES])
    rid = lax.broadcasted_iota(jnp.int32, u.shape, 0)
    u1 = jnp.where(rid == 0, prev2[1:2], pltpu.roll(u, 1, axis=0))
    u2 = jnp.where(rid == 0, prev2[0:1], jnp.where(rid == 1, prev2[1:2], pltpu.roll(u, 2, axis=0)))
    cw = cw_ref[...]
    y = cw[0:1] * u2 + cw[1:2] * u1 + cw[2:3] * u
    y_ref[0] = (bg_ref[0] * y).astype(y_ref.dtype)

    @pl.when(s == pl.num_programs(2) - 1)
    def _():
        st_ref[0] = u[ts - (CONV_WIDTH - 1):ts]


def _conv(z_all, buf, conv_w, layer):
    B, S, _ = z_all.shape
    ts = _pick(S, (512, 256, 128, 64, 32, 16, 8))
    nb8 = ts // SUBLANES
    nj = MIX_WIDTH // COL_BLK
    c0 = OFF_CONV // COL_BLK
    blk = lambda off: pl.BlockSpec((1, ts, COL_BLK), lambda b, j, s, off=off: (b, s, off + j))
    pblk = lambda off: pl.BlockSpec(
        (1, SUBLANES, COL_BLK), lambda b, j, s, off=off: (b, jnp.maximum(s * nb8 - 1, 0), off + j))
    st_spec = pl.BlockSpec((1, CONV_WIDTH - 1, COL_BLK), lambda b, j, s: (b, 0, j))
    return pl.pallas_call(
        _conv_kernel,
        out_shape=[jax.ShapeDtypeStruct((B, S, MIX_WIDTH), BF16),
                   jax.ShapeDtypeStruct((B, CONV_WIDTH - 1, MIX_WIDTH), F32)],
        grid=(B, nj, S // ts),
        in_specs=[blk(c0), blk(c0 + nj), blk(c0 + 2 * nj), pblk(c0 + nj), pblk(c0 + 2 * nj), st_spec,
                  pl.BlockSpec((None, CONV_WIDTH, COL_BLK), lambda b, j, s: (layer, 0, j))],
        out_specs=[pl.BlockSpec((1, ts, COL_BLK), lambda b, j, s: (b, s, j)), st_spec],
        compiler_params=_cparams(("parallel", "parallel", "arbitrary")), name="gated_conv",
    )(z_all, z_all, z_all, z_all, z_all, buf, conv_w)


def _attn_core(q_refs, k_all, v_all, sinks_ref, q0, k0, o_ref):
    tq = q_refs[0].shape[1]
    tk = k_all.shape[0]
    qpos = q0 + lax.broadcasted_iota(jnp.int32, (tq, tk), 0)
    kpos = k0 + lax.broadcasted_iota(jnp.int32, (tq, tk), 1)
    dc = (qpos >> CHUNK_SHIFT) - (kpos >> CHUNK_SHIFT)
    valid = (dc >= 0) & (dc <= W_CHUNKS) & (kpos >= 0)
    dist = jnp.abs(qpos - kpos).astype(F32)
    heads_per_blk = COL_BLK // HEAD_DIM
    for g in range(KV_HEADS):
        kg = k_all[:, g * HEAD_DIM:(g + 1) * HEAD_DIM].astype(BF16)
        vg = v_all[:, g * HEAD_DIM:(g + 1) * HEAD_DIM].astype(BF16)
        for j in range(GROUP):
            h = g * GROUP + j
            lo = (h % heads_per_blk) * HEAD_DIM
            qh = q_refs[h // heads_per_blk][0, :, lo:lo + HEAD_DIM].astype(BF16)
            s = lax.dot_general(qh, kg, (((1,), (1,)), ((), ())), preferred_element_type=F32)
            slope = 2.0 ** (-8.0 * (h + 1) / N_HEADS)
            s = jnp.where(valid, s * ATTN_SCALE - slope * dist, -1e30)
            sk = sinks_ref[h]
            m = jnp.maximum(jnp.max(s, axis=-1, keepdims=True), sk)
            p = jnp.exp(s - m)
            den = jnp.sum(p, axis=-1, keepdims=True) + jnp.exp(sk - m)
            o = jnp.dot(p.astype(BF16), vg, preferred_element_type=F32) / den
            o_ref[0, :, h * HEAD_DIM:(h + 1) * HEAD_DIM] = o.astype(o_ref.dtype)


def _attn_prompt_kernel(sinks_ref, q0_ref, q1_ref, q2_ref, q3_ref, ka_ref, kb_ref, kc_ref,
                        va_ref, vb_ref, vc_ref, o_ref):
    n = pl.program_id(1)
    k_all = jnp.concatenate([ka_ref[0], kb_ref[0], kc_ref[0]], axis=0)
    v_all = jnp.concatenate([va_ref[0], vb_ref[0], vc_ref[0]], axis=0)
    _attn_core((q0_ref, q1_ref, q2_ref, q3_ref), k_all, v_all, sinks_ref,
               n * CHUNK, (n - W_CHUNKS) * CHUNK, o_ref)


def _attn_prompt(z_all, sinks):
    B, S, _ = z_all.shape
    qb = OFF_ATTN // COL_BLK
    kb = qb + MIX_WIDTH // COL_BLK
    vb = kb + 1
    qspec = lambda i: pl.BlockSpec((1, CHUNK, COL_BLK), lambda b, n, s, i=i: (b, n, qb + i))
    wspec = lambda back, col: pl.BlockSpec(
        (1, CHUNK, COL_BLK), lambda b, n, s, back=back, col=col: (b, jnp.maximum(n - back, 0), col))
    grid_spec = pltpu.PrefetchScalarGridSpec(
        num_scalar_prefetch=1, grid=(B, S // CHUNK),
        in_specs=[qspec(0), qspec(1), qspec(2), qspec(3),
                  wspec(2, kb), wspec(1, kb), wspec(0, kb), wspec(2, vb), wspec(1, vb), wspec(0, vb)],
        out_specs=pl.BlockSpec((1, CHUNK, MIX_WIDTH), lambda b, n, s: (b, n, 0)))
    return pl.pallas_call(
        _attn_prompt_kernel, out_shape=jax.ShapeDtypeStruct((B, S, MIX_WIDTH), BF16),
        grid_spec=grid_spec, compiler_params=_cparams(("parallel", "parallel")), name="swa_prompt",
    )(sinks, *([z_all] * 10))


def _attn_sample_kernel(sinks_ref, q0_ref, q1_ref, q2_ref, q3_ref, kc_ref, kn_ref, vc_ref, vn_ref,
                        o_ref, *, pos0):
    L = kc_ref.shape[1]
    k_all = jnp.concatenate([kc_ref[0], kn_ref[0]], axis=0)
    v_all = jnp.concatenate([vc_ref[0], vn_ref[0]], axis=0)
    _attn_core((q0_ref, q1_ref, q2_ref, q3_ref), k_all, v_all, sinks_ref, pos0, pos0 - L, o_ref)


def _attn_sample(z_all, k_cache, v_cache, sinks, pos0):
    B, S, _ = z_all.shape
    L = k_cache.shape[1]
    qb = OFF_ATTN // COL_BLK
    kb = qb + MIX_WIDTH // COL_BLK
    vb = kb + 1
    qspec = lambda i: pl.BlockSpec((1, S, COL_BLK), lambda b, s, i=i: (b, 0, qb + i))
    nspec = lambda col: pl.BlockSpec((1, S, COL_BLK), lambda b, s, col=col: (b, 0, col))
    cspec = pl.BlockSpec((1, L, KV_WIDTH), lambda b, s: (b, 0, 0))
    grid_spec = pltpu.PrefetchScalarGridSpec(
        num_scalar_prefetch=1, grid=(B,),
        in_specs=[qspec(0), qspec(1), qspec(2), qspec(3), cspec, nspec(kb), cspec, nspec(vb)],
        out_specs=pl.BlockSpec((1, S, MIX_WIDTH), lambda b, s: (b, 0, 0)))
    return pl.pallas_call(
        functools.partial(_attn_sample_kernel, pos0=pos0),
        out_shape=jax.ShapeDtypeStruct((B, S, MIX_WIDTH), BF16),
        grid_spec=grid_spec, compiler_params=_cparams(("parallel",)), name="swa_sample",
    )(sinks, z_all, z_all, z_all, z_all, k_cache, z_all, v_cache, z_all)


def _merge_kernel(ya_ref, yb_ref, yc_ref, wa_ref, wb_ref, wc_ref, ga_ref, gb_ref, gc_ref, o_ref, *, tm):
    acc = None
    for y_ref, w_ref, g_ref in ((ya_ref, wa_ref, ga_ref), (yb_ref, wb_ref, gb_ref), (yc_ref, wc_ref, gc_ref)):
        y = y_ref[...].reshape(tm, y_ref.shape[-1])
        gate = jax.nn.sigmoid(g_ref[...].reshape(tm, g_ref.shape[-1]))
        c = gate * jnp.dot(y, w_ref[...].astype(BF16), preferred_element_type=F32)
        acc = c if acc is None else acc + c
    o_ref[...] = acc.reshape(o_ref.shape).astype(o_ref.dtype)


def _merge(ya, yb, yc, w0p, w_branch_out, layer, z_all, D):
    B, S, _ = ya.shape
    bb, ts = _row_tiles(B, S, 1024)
    tm = bb * ts
    tn = 256
    g0 = OFF_GATE // tn
    nd = D // tn
    yspec = pl.BlockSpec((bb, ts, MIX_WIDTH), lambda b, m, n: (b, m, 0))
    wspec = lambda r: pl.BlockSpec((None, None, MIX_WIDTH, tn), lambda b, m, n, r=r: (layer, r, 0, n))
    gspec = lambda r: pl.BlockSpec((bb, ts, tn), lambda b, m, n, r=r: (b, m, g0 + r * nd + n))
    return pl.pallas_call(
        functools.partial(_merge_kernel, tm=tm),
        out_shape=jax.ShapeDtypeStruct((B, S, D), BF16),
        grid=(B // bb, S // ts, nd),
        in_specs=[yspec, yspec, yspec,
                  pl.BlockSpec((MIX_WIDTH, tn), lambda b, m, n: (0, n)), wspec(1), wspec(2),
                  gspec(0), gspec(1), gspec(2)],
        out_specs=pl.BlockSpec((bb, ts, tn), lambda b, m, n: (b, m, n)),
        compiler_params=_cparams(("parallel", "parallel", "parallel")), name="branch_merge",
    )(ya, yb, yc, w0p, w_branch_out, w_branch_out, z_all, z_all, z_all)


def _im_cols(w):
    lead = w.shape[:-1]
    return jnp.swapaxes(w.reshape(lead + (N_HEADS, HEAD_DIM)), -1, -2).reshape(lead + (MIX_WIDTH,))


def _im_cols_inv(w):
    lead = w.shape[:-1]
    return jnp.swapaxes(w.reshape(lead + (HEAD_DIM, N_HEADS)), -1, -2).reshape(lead + (MIX_WIDTH,))


def _pack_rwkv_cols(t):
    lead = t.shape[:-1]
    z32 = jnp.zeros(lead + (LANES - DECAY_LORA,), t.dtype)
    o = 3 * MIX_WIDTH
    return jnp.concatenate([
        _im_cols(t[..., 0:MIX_WIDTH]), _im_cols(t[..., MIX_WIDTH:2 * MIX_WIDTH]),
        _im_cols(t[..., 2 * MIX_WIDTH:3 * MIX_WIDTH]),
        t[..., o:o + DECAY_LORA], z32,
        t[..., o + DECAY_LORA:o + DECAY_LORA + ICLR_LORA], z32,
        t[..., o + DECAY_LORA + ICLR_LORA:]], axis=-1)


def _unpack_rwkv_cols(t):
    o = OFF_LORA
    return jnp.concatenate([
        _im_cols_inv(t[..., 0:MIX_WIDTH]), _im_cols_inv(t[..., MIX_WIDTH:2 * MIX_WIDTH]),
        _im_cols_inv(t[..., 2 * MIX_WIDTH:3 * MIX_WIDTH]),
        t[..., o:o + DECAY_LORA], t[..., o + LANES:o + LANES + ICLR_LORA],
        t[..., o + 2 * LANES:o + 2 * LANES + GATE_LORA]], axis=-1)


def _pad_rows(w, rows):
    return jnp.concatenate([w, jnp.zeros((rows - w.shape[0],) + w.shape[1:], w.dtype)], axis=0)


def _state_to_tiles(s):
    B = s.shape[0]
    s = s.reshape(B, N_HEADS, N_VH, SUBLANES, N_KH, 4)
    s = jnp.transpose(s, (0, 2, 4, 3, 5, 1))
    return s.reshape(B, N_STATE_TILES, SUBLANES, LANES)


def _tiles_to_state(s):
    B = s.shape[0]
    s = s.reshape(B, N_VH, N_KH, SUBLANES, 4, N_HEADS)
    s = jnp.transpose(s, (0, 5, 1, 3, 2, 4))
    return s.reshape(B, N_HEADS, HEAD_DIM, HEAD_DIM)


def _mixers(i, x, m, st, pos0, lw, P):
    B, S, D = x.shape
    h = _norm(x, P["norm_mix_g"], i, m[1], m[0], name="norm_mix")
    z_all = _mm(h, [lw["w_all"]], [()], tm_max=1024, tn=512, tk=D, name="in_proj")

    if st is None:
        s0 = jnp.zeros((B, N_STATE_TILES, SUBLANES, LANES), F32)
        shift0 = jnp.zeros((B, 1, RWKV_PAD), F32)
        conv0 = jnp.zeros((B, CONV_WIDTH - 1, MIX_WIDTH), F32)
    else:
        s0 = _state_to_tiles(st[0].astype(F32))
        shift0 = _pack_rwkv_cols(st[1].astype(F32))
        conv0 = st[2].astype(F32)

    a, wr, w, b, k, v, g, br, kr, rkr = _rwkv_prep(z_all, shift0, lw)
    vf = jnp.tile(v.reshape(B, S, HEAD_DIM, N_HEADS), (1, 1, 1, LANES // N_HEADS))
    y4, s_tiles = _rwkv_scan(a, wr, w, b, k, vf, br, kr, s0)
    y = y4[..., :N_HEADS].reshape(B, S, MIX_WIDTH)
    y_a = _rwkv_post(y, v, g, rkr, lw)
    s_new = _tiles_to_state(s_tiles)
    shift_new = _unpack_rwkv_cols(z_all[:, S - 1:S, 0:RWKV_PAD])

    y_b, conv_new = _conv(z_all, conv0, P["conv_w"], i)

    k_new_all = z_all[:, :, OFF_ATTN + MIX_WIDTH:OFF_ATTN + MIX_WIDTH + KV_WIDTH]
    v_new_all = z_all[:, :, OFF_ATTN + MIX_WIDTH + KV_WIDTH:OFF_ATTN + MIX_WIDTH + 2 * KV_WIDTH]
    if st is None:
        y_c = _attn_prompt(z_all, lw["sinks"])
        k_new = k_new_all[:, S - WINDOW:].reshape(B, WINDOW, KV_HEADS, HEAD_DIM)
        v_new = v_new_all[:, S - WINDOW:].reshape(B, WINDOW, KV_HEADS, HEAD_DIM)
    else:
        L = st[3].shape[1]
        kc = st[3].astype(F32).reshape(B, L, KV_WIDTH)
        vc = st[4].astype(F32).reshape(B, L, KV_WIDTH)
        y_c = _attn_sample(z_all, kc, vc, lw["sinks"], pos0)
        k_new = jnp.concatenate([kc, k_new_all], axis=1)[:, -L:].reshape(B, L, KV_HEADS, HEAD_DIM)
        v_new = jnp.concatenate([vc, v_new_all], axis=1)[:, -L:].reshape(B, L, KV_HEADS, HEAD_DIM)

    merged = _merge(y_a, y_b, y_c, lw["wb0"], P["w_branch_out"], i, z_all, D)
    x = _mm(merged, [P["w_mix_out"]], [(i,)], tm_max=1024, tn=512, tk=D, epi="resgate",
            extras=(x, m[2]), name="mix_out")
    return x, (s_new, shift_new, conv_new, k_new, v_new)


def _channel_mixers(i, xs, ms, lw, P):
    j = i // 2
    if i % 2 == 0:
        outs = []
        for x, m in zip(xs, ms):
            h = _norm(x, P["norm_ffn_g"], i, m[4], m[3], name="norm_ffn")
            act = _mm(h, [P["ffn_w_gate"], P["ffn_w_up"]], [(j,), (j,)], tm_max=2048, tn=256,
                      tk=h.shape[-1], epi="swiglu", out_dtype=BF16, name="ffn_up")
            outs.append(_mm(act, [P["ffn_w_down"]], [(j,)], tm_max=1024, tn=1024, tk=2048, epi="resgate",
                            extras=(x, m[5]), name="ffn_down"))
        return outs
    n_exp = P["moe_w_gate"].shape[1]
    hs, wsels, isels = [], [], []
    for x, m in zip(xs, ms):
        h, logits = _norm(x, P["norm_ffn_g"], i, m[4], m[3], router=lw["router"], out_dtype=F32,
                          name="norm_moe")
        wsel, isel = _topk_route(logits, n_exp)
        hs.append(h)
        wsels.append(wsel)
        isels.append(isel)
    return _moe(hs, wsels, isels, xs, [m[5] for m in ms], P["moe_w_gate"], P["moe_w_up"],
                P["moe_w_down"], j)


def kernel(x_prompt, x_sample, c_prompt, c_sample, state_rwkv, state_rwkv_shift, state_conv, cache_swa_k, cache_swa_v, ada_w, ada_b, norm_mix_g, norm_ffn_g, w_in, rwkv_mu, rwkv_w_up, rwkv_w0, rwkv_a_up, rwkv_a0, rwkv_g_up, rwkv_k_k, rwkv_k_a, rwkv_r_k, rwkv_gn_w, rwkv_gn_b, conv_w, attn_sinks, w_branch_out, w_mix_out, ffn_w_gate, ffn_w_up, ffn_w_down, moe_router, moe_w_gate, moe_w_up, moe_w_down, final_norm_g):
    depth, D = norm_mix_g.shape
    Bp, Sp, _ = x_prompt.shape
    Bs, Ss, _ = x_sample.shape

    P = dict(norm_mix_g=norm_mix_g.reshape(depth, 1, D), norm_ffn_g=norm_ffn_g.reshape(depth, 1, D),
             conv_w=conv_w, w_branch_out=w_branch_out, w_mix_out=w_mix_out,
             ffn_w_gate=ffn_w_gate, ffn_w_up=ffn_w_up, ffn_w_down=ffn_w_down,
             moe_w_gate=moe_w_gate, moe_w_up=moe_w_up, moe_w_down=moe_w_down)

    nc = Bp + Bs
    nc_pad = -(-nc // 16) * 16
    c_all = jnp.concatenate([c_prompt, c_sample, jnp.zeros((nc_pad - nc, D), F32)], axis=0)[None]
    ada_b3 = ada_b.reshape(depth, 1, 6 * D)

    xp, xs = x_prompt, x_sample
    outs_p, outs_s = [], []
    for i in range(depth):
        mod = _mm(c_all, [ada_w], [(i,)], tm_max=2048, tn=1024, tk=2048, epi="bias", extras=(ada_b3, i),
                  silu_in=True, name="ada_mod")[0]
        mod_p = mod[:Bp].reshape(Bp, 6, D)
        mod_s = mod[Bp:nc].reshape(Bs, 6, D)
        ms_p = [mod_p[:, j:j + 1] for j in range(6)]
        ms_s = [mod_s[:, j:j + 1] for j in range(6)]

        o = 3 * MIX_WIDTH
        wi = w_in[i]
        zc = jnp.zeros((D, LANES - DECAY_LORA), BF16)
        cb = lambda t: t.astype(BF16)
        w_all = jnp.concatenate([
            cb(_im_cols(wi[:, 0:MIX_WIDTH])), cb(_im_cols(wi[:, MIX_WIDTH:2 * MIX_WIDTH])),
            cb(_im_cols(wi[:, 2 * MIX_WIDTH:3 * MIX_WIDTH])),
            cb(wi[:, o:o + DECAY_LORA]), zc, cb(wi[:, o + DECAY_LORA:o + DECAY_LORA + ICLR_LORA]), zc,
            cb(wi[:, o + DECAY_LORA + ICLR_LORA:])], axis=1)
        vec = lambda t: _im_cols(t[i]).reshape(1, 1, MIX_WIDTH)
        lw = dict(
            w_all=w_all,
            mu=_pack_rwkv_cols(rwkv_mu[i]).reshape(1, 1, RWKV_PAD),
            w_up=_pad_rows(_im_cols(rwkv_w_up[i]), LANES)[None], w0=vec(rwkv_w0),
            a_up=_pad_rows(_im_cols(rwkv_a_up[i]), LANES)[None], a0=vec(rwkv_a0),
            g_up=_im_cols(rwkv_g_up[i])[None],
            k_k=vec(rwkv_k_k), k_a=vec(rwkv_k_a), r_k=vec(rwkv_r_k),
            gn_w=vec(rwkv_gn_w), gn_b=vec(rwkv_gn_b),
            sinks=attn_sinks[i],
            wb0=jnp.swapaxes(w_branch_out[i, 0].reshape(N_HEADS, HEAD_DIM, D), 0, 1).reshape(MIX_WIDTH, D),
        )
        if i % 2 == 1:
            r = moe_router[i // 2]
            lw["router"] = jnp.concatenate([r, jnp.zeros((D, LANES - r.shape[1]), F32)], axis=1)

        xp, st_p = _mixers(i, xp, ms_p, None, 0, lw, P)
        st_s_in = (state_rwkv[i], state_rwkv_shift[i], state_conv[i], cache_swa_k[i], cache_swa_v[i])
        xs, st_s = _mixers(i, xs, ms_s, st_s_in, PAST_LEN, lw, P)
        xp, xs = _channel_mixers(i, [xp, xs], [ms_p, ms_s], lw, P)
        outs_p.append(st_p)
        outs_s.append(st_s)

    fg = final_norm_g.reshape(1, 1, D)
    y_prompt = _norm(xp, fg, 0, out_dtype=F32, name="norm_final")
    y_sample = _norm(xs, fg, 0, out_dtype=F32, name="norm_final")
    stack = lambda outs, idx: jnp.stack([o[idx] for o in outs])
    return (y_prompt, y_sample,
            stack(outs_p, 0), stack(outs_p, 1), stack(outs_p, 2), stack(outs_p, 3), stack(outs_p, 4),
            stack(outs_s, 0), stack(outs_s, 1), stack(outs_s, 2), stack(outs_s, 3), stack(outs_s, 4))
```

```python
import functools
import math

import jax
import jax.numpy as jnp
from jax import lax
from jax.experimental import pallas as pl
from jax.experimental.pallas import tpu as pltpu

F32 = jnp.float32
BF16 = jnp.bfloat16

HEAD_DIM = 64
N_HEADS = 32
MIX_WIDTH = N_HEADS * HEAD_DIM
KV_HEADS = 8
GROUP = N_HEADS // KV_HEADS
KV_WIDTH = KV_HEADS * HEAD_DIM
CHUNK = 64
CHUNK_SHIFT = 6
W_CHUNKS = 2
WINDOW = W_CHUNKS * CHUNK
PAST_LEN = 1024
CONV_WIDTH = 3
DECAY_LORA = 96
ICLR_LORA = 96
GATE_LORA = 256
N_BRANCH = 3
TOP_K = 2
RMS_EPS = 1e-5
GN_EPS = 64e-5
ATTN_SCALE = HEAD_DIM ** -0.5

LANES = 128
SUBLANES = 8
VMEM_LIMIT_BYTES = 56 * 1024 * 1024

LORA_PAD = 512
OFF_R, OFF_K, OFF_V = 0, MIX_WIDTH, 2 * MIX_WIDTH
OFF_LORA = 3 * MIX_WIDTH
RWKV_PAD = OFF_LORA + LORA_PAD
OFF_CONV = RWKV_PAD
OFF_ATTN = OFF_CONV + 3 * MIX_WIDTH
OFF_GATE = OFF_ATTN + MIX_WIDTH + 2 * KV_WIDTH
COL_BLK = 512
RWKV_COLS = 3 * MIX_WIDTH + DECAY_LORA + ICLR_LORA + GATE_LORA


def _cparams(sem):
    return pltpu.CompilerParams(dimension_semantics=sem, vmem_limit_bytes=VMEM_LIMIT_BYTES)


def _pick(n, cands):
    for c in cands:
        if n % c == 0:
            return c
    return n


def _row_tiles(B, S, tm_max):
    if S >= tm_max:
        ts = _pick(S, (tm_max, tm_max // 2, tm_max // 4, 256, 128, 64, 32, 16, 8))
        return 1, ts
    bb = max(1, min(B, tm_max // S))
    while B % bb:
        bb -= 1
    return bb, S


def _mm_kernel(*refs, n_w, epi, tm, nk, silu_in):
    x_ref = refs[0]
    w_refs = refs[1:1 + n_w]
    pos = 1 + n_w
    n_extra = {"none": 0, "swiglu": 0, "bias": 1, "resgate": 2}[epi]
    extra = refs[pos:pos + n_extra]
    pos += n_extra
    o_ref = refs[pos]
    acc_refs = refs[pos + 1:]

    x = x_ref[...]
    x = x.reshape(tm, x.shape[-1])
    if silu_in:
        xf = x.astype(F32)
        x = xf * jax.nn.sigmoid(xf)
    x = x.astype(BF16)

    def finish(accs):
        a = accs[0]
        if epi == "swiglu":
            r = (a * jax.nn.sigmoid(a)) * accs[1]
        elif epi == "bias":
            r = a + extra[0][...]
        elif epi == "resgate":
            r = extra[0][...] + extra[1][...] * a.reshape(o_ref.shape)
        else:
            r = a
        o_ref[...] = r.reshape(o_ref.shape).astype(o_ref.dtype)

    dots = [jnp.dot(x, w_refs[j][...].astype(BF16), preferred_element_type=F32) for j in range(n_w)]
    if nk == 1:
        finish(dots)
        return

    k = pl.program_id(3)
    if not acc_refs:
        acc_refs = (o_ref,)

    @pl.when(k == 0)
    def _():
        for acc in acc_refs:
            acc[...] = jnp.zeros_like(acc)

    for j in range(n_w):
        acc_refs[j][...] += dots[j].reshape(acc_refs[j].shape)

    if epi != "none" or acc_refs[0] is not o_ref:
        @pl.when(k == nk - 1)
        def _():
            finish([acc[...] for acc in acc_refs])


def _mm(x, ws, leads, *, tm_max, tn, tk, epi="none", out_dtype=F32, extras=(), silu_in=False,
        name="mm"):
    B, S, K = x.shape
    N = ws[0].shape[-1]
    bb, ts = _row_tiles(B, S, tm_max)
    tm = bb * ts
    tn = _pick(N, (tn, 512, 256, 128))
    tk = _pick(K, (tk, 1024, 512, 256, 128))
    nk = K // tk
    grid = (B // bb, S // ts, N // tn, nk)

    in_specs = [pl.BlockSpec((bb, ts, tk), lambda b, m, n, k: (b, m, k))]
    for w, lead in zip(ws, leads):
        nl = len(lead)
        in_specs.append(pl.BlockSpec((None,) * nl + (tk, tn),
                                     lambda b, m, n, k, lead=lead: tuple(lead) + (k, n)))
    if epi == "bias":
        in_specs.append(pl.BlockSpec((None, 1, tn), lambda b, m, n, k, l=extras[1]: (l, 0, n)))
        extras = extras[:1]
    elif epi == "resgate":
        in_specs.append(pl.BlockSpec((bb, ts, tn), lambda b, m, n, k: (b, m, n)))
        in_specs.append(pl.BlockSpec((bb, 1, tn), lambda b, m, n, k: (b, 0, n)))
    n_w = len(ws)
    direct = nk == 1 or (n_w == 1 and out_dtype == F32)
    kern = functools.partial(_mm_kernel, n_w=n_w, epi=epi, tm=tm, nk=nk, silu_in=silu_in)
    return pl.pallas_call(
        kern,
        out_shape=jax.ShapeDtypeStruct((B, S, N), out_dtype),
        grid=grid,
        in_specs=in_specs,
        out_specs=pl.BlockSpec((bb, ts, tn), lambda b, m, n, k: (b, m, n)),
        scratch_shapes=[] if direct else [pltpu.VMEM((tm, tn), F32) for _ in range(n_w)],
        compiler_params=_cparams(("parallel", "parallel", "parallel", "arbitrary")),
        name=name,
    )(x, *ws, *extras)


def _norm_kernel(*refs, modulate, route):
    x_ref, g_ref = refs[0], refs[1]
    pos = 2
    if modulate:
        sc_ref, sh_ref = refs[2], refs[3]
        pos = 4
    if route:
        r_ref = refs[pos]
        pos += 1
    o_ref = refs[pos]
    x = x_ref[...]
    ms = jnp.mean(x * x, axis=-1, keepdims=True)
    y = x * lax.rsqrt(ms + RMS_EPS) * g_ref[...]
    if modulate:
        y = y * (1.0 + sc_ref[...]) + sh_ref[...]
    o_ref[...] = y.astype(o_ref.dtype)
    if route:
        l_ref = refs[pos + 1]
        y2 = y.reshape(-1, y.shape[-1])
        logits = jnp.dot(y2, r_ref[...], preferred_element_type=F32, precision=lax.Precision.HIGHEST)
        l_ref[...] = logits.reshape(l_ref.shape)


def _norm(x, g3, lead, scale=None, shift=None, router=None, out_dtype=BF16, name="norm"):
    B, S, D = x.shape
    bb, ts = _row_tiles(B, S, 256)
    grid = (B // bb, S // ts)
    modulate = scale is not None
    route = router is not None
    in_specs = [pl.BlockSpec((bb, ts, D), lambda b, m: (b, m, 0)),
                pl.BlockSpec((None, 1, D), lambda b, m: (lead, 0, 0))]
    args = [x, g3]
    if modulate:
        in_specs += [pl.BlockSpec((bb, 1, D), lambda b, m: (b, 0, 0))] * 2
        args += [scale, shift]
    out_shape = [jax.ShapeDtypeStruct((B, S, D), out_dtype)]
    out_specs = [pl.BlockSpec((bb, ts, D), lambda b, m: (b, m, 0))]
    if route:
        in_specs.append(pl.BlockSpec((D, LANES), lambda b, m: (0, 0)))
        args.append(router)
        out_shape.append(jax.ShapeDtypeStruct((B, S, LANES), F32))
        out_specs.append(pl.BlockSpec((bb, ts, LANES), lambda b, m: (b, m, 0)))
    res = pl.pallas_call(
        functools.partial(_norm_kernel, modulate=modulate, route=route),
        out_shape=out_shape, grid=grid, in_specs=in_specs, out_specs=out_specs,
        compiler_params=_cparams(("parallel", "parallel")), name=name,
    )(*args)
    return res if route else res[0]


def _topk_kernel(l_ref, w_ref, i_ref, *, n_exp):
    l = l_ref[...]
    lane = lax.broadcasted_iota(jnp.int32, l.shape, l.ndim - 1)
    neg = jnp.float32(-jnp.inf)
    l = jnp.where(lane < n_exp, l, neg)
    m1 = jnp.max(l, axis=-1, keepdims=True)
    i1 = jnp.min(jnp.where(l == m1, lane, LANES), axis=-1, keepdims=True)
    l2 = jnp.where(lane == i1, neg, l)
    m2 = jnp.max(l2, axis=-1, keepdims=True)
    i2 = jnp.min(jnp.where(l2 == m2, lane, LANES), axis=-1, keepdims=True)
    e = jnp.exp(m2 - m1)
    den = 1.0 + e
    w_ref[...] = jnp.where(lane == 0, 1.0 / den, jnp.where(lane == 1, e / den, 0.0))
    i_ref[...] = jnp.where(lane == 0, i1, jnp.where(lane == 1, i2, 0))


def _topk_route(logits, n_exp):
    B, S, _ = logits.shape
    bb, ts = _row_tiles(B, S, 1024)
    spec = pl.BlockSpec((bb, ts, LANES), lambda b, m: (b, m, 0))
    return pl.pallas_call(
        functools.partial(_topk_kernel, n_exp=n_exp),
        out_shape=[jax.ShapeDtypeStruct(logits.shape, F32), jax.ShapeDtypeStruct(logits.shape, jnp.int32)],
        grid=(B // bb, S // ts), in_specs=[spec], out_specs=[spec, spec],
        compiler_params=_cparams(("parallel", "parallel")), name="topk_route",
    )(logits)


MOE_TILE = 1024
GATHER_ROWS = 256


def _row_copy(src_hbm, dst_vmem, sem, src_row, dst_row):
    return pltpu.make_async_copy(src_hbm.at[pl.ds(src_row, 1), :], dst_vmem.at[pl.ds(dst_row, 1), :], sem)


def _gather_kernel(tok_ref, nu_ref, h_hbm, o_ref, buf, sem, *, rows):
    base = pl.program_id(0) * rows

    @pl.when(base < nu_ref[0] * MOE_TILE)
    def _():
        def start(r, c):
            _row_copy(h_hbm, buf, sem, tok_ref[base + r], r).start()
            return c

        def wait(r, c):
            _row_copy(h_hbm, buf, sem, 0, r).wait()
            return c

        lax.fori_loop(0, rows, start, 0, unroll=8)
        lax.fori_loop(0, rows, wait, 0, unroll=8)
        o_ref[...] = buf[...].astype(o_ref.dtype)

    @pl.when(base >= nu_ref[0] * MOE_TILE)
    def _():
        o_ref[...] = jnp.zeros_like(o_ref)


def _gather_rows(h, row_token, n_used, n_rows):
    T, D = h.shape
    rows = GATHER_ROWS
    grid_spec = pltpu.PrefetchScalarGridSpec(
        num_scalar_prefetch=2, grid=(n_rows // rows,),
        in_specs=[pl.BlockSpec(memory_space=pl.ANY)],
        out_specs=pl.BlockSpec((rows, D), lambda i, tok, nu: (i, 0)),
        scratch_shapes=[pltpu.VMEM((rows, D), F32), pltpu.SemaphoreType.DMA])
    return pl.pallas_call(
        functools.partial(_gather_kernel, rows=rows),
        out_shape=jax.ShapeDtypeStruct((n_rows, D), BF16), grid_spec=grid_spec,
        compiler_params=_cparams(("arbitrary",)), name="moe_gather",
    )(row_token, n_used, h)


def _gup_kernel(te_ref, nu_ref, x_ref, wg_ref, wu_ref, o_ref):
    m = pl.program_id(0)

    @pl.when(m < nu_ref[0])
    def _():
        x = x_ref[...]
        g = jnp.dot(x, wg_ref[...].astype(BF16), preferred_element_type=F32)
        u = jnp.dot(x, wu_ref[...].astype(BF16), preferred_element_type=F32)
        o_ref[...] = ((g * jax.nn.sigmoid(g)) * u).astype(o_ref.dtype)

    @pl.when(m >= nu_ref[0])
    def _():
        o_ref[...] = jnp.zeros_like(o_ref)


def _grouped_up(xg, wg, wu, j, tile_expert, n_used):
    P, D = xg.shape
    F = wg.shape[-1]
    tm = MOE_TILE
    tn = _pick(F, (256, 128))
    nn = F // tn

    def wmap(m, n, te, nu):
        return (j, te[m], 0, jnp.where(m < nu[0], n, nn - 1))

    grid_spec = pltpu.PrefetchScalarGridSpec(
        num_scalar_prefetch=2, grid=(P // tm, nn),
        in_specs=[pl.BlockSpec((tm, D), lambda m, n, te, nu: (jnp.minimum(m, nu[0] - 1), 0)),
                  pl.BlockSpec((None, None, D, tn), wmap), pl.BlockSpec((None, None, D, tn), wmap)],
        out_specs=pl.BlockSpec((tm, tn), lambda m, n, te, nu: (m, n)))
    return pl.pallas_call(
        _gup_kernel, out_shape=jax.ShapeDtypeStruct((P, F), BF16), grid_spec=grid_spec,
        compiler_params=_cparams(("arbitrary", "arbitrary")), name="moe_up",
    )(tile_expert, n_used, xg, wg, wu)


def _gdown_kernel(te_ref, nu_ref, x_ref, w_ref, o_ref):
    m = pl.program_id(0)
    k = pl.program_id(2)

    @pl.when(k == 0)
    def _():
        o_ref[...] = jnp.zeros_like(o_ref)

    @pl.when(m < nu_ref[0])
    def _():
        o_ref[...] += jnp.dot(x_ref[...], w_ref[...].astype(BF16), preferred_element_type=F32)


def _grouped_down(act, wd, j, tile_expert, n_used):
    P, F = act.shape
    D = wd.shape[-1]
    tm = MOE_TILE
    tn = _pick(D, (2048, 1024, 512, 256, 128))
    tk = _pick(F, (1024, 512, 256, 128))
    nn, nk = D // tn, F // tk

    def live(m, nu, idx, last):
        return jnp.where(m < nu[0], idx, last)

    grid_spec = pltpu.PrefetchScalarGridSpec(
        num_scalar_prefetch=2, grid=(P // tm, nn, nk),
        in_specs=[pl.BlockSpec((tm, tk), lambda m, n, k, te, nu: (jnp.minimum(m, nu[0] - 1), live(m, nu, k, nk - 1))),
                  pl.BlockSpec((None, None, tk, tn),
                               lambda m, n, k, te, nu: (j, te[m], live(m, nu, k, nk - 1), live(m, nu, n, nn - 1)))],
        out_specs=pl.BlockSpec((tm, tn), lambda m, n, k, te, nu: (m, n)))
    return pl.pallas_call(
        _gdown_kernel, out_shape=jax.ShapeDtypeStruct((P, D), F32), grid_spec=grid_spec,
        compiler_params=_cparams(("arbitrary", "arbitrary", "arbitrary")), name="moe_down",
    )(tile_expert, n_used, act, wd)


def _combine_kernel(pos_ref, og_hbm, w_ref, x_ref, g_ref, o_ref, buf, sem, *, rows, base):
    t0 = base + (pl.program_id(0) * pl.num_programs(1) + pl.program_id(1)) * rows

    def start(r, c):
        for jj in range(TOP_K):
            _row_copy(og_hbm, buf.at[jj], sem, pos_ref[TOP_K * (t0 + r) + jj], r).start()
        return c

    def wait(r, c):
        for jj in range(TOP_K):
            _row_copy(og_hbm, buf.at[jj], sem, 0, r).wait()
        return c

    lax.fori_loop(0, rows, start, 0)
    lax.fori_loop(0, rows, wait, 0)
    w = w_ref[0]
    f = w[:, 0:1] * buf[0] + w[:, 1:2] * buf[1]
    o_ref[0] = x_ref[0] + g_ref[0] * f


def _combine(og, pos, wsel, x, gate_f, base):
    B, S, D = x.shape
    rows = _pick(S, (256, 128, 64, 32, 16, 8))
    grid_spec = pltpu.PrefetchScalarGridSpec(
        num_scalar_prefetch=1, grid=(B, S // rows),
        in_specs=[pl.BlockSpec(memory_space=pl.ANY),
                  pl.BlockSpec((1, rows, LANES), lambda b, s, p: (b, s, 0)),
                  pl.BlockSpec((1, rows, D), lambda b, s, p: (b, s, 0)),
                  pl.BlockSpec((1, 1, D), lambda b, s, p: (b, 0, 0))],
        out_specs=pl.BlockSpec((1, rows, D), lambda b, s, p: (b, s, 0)),
        scratch_shapes=[pltpu.VMEM((TOP_K, rows, D), F32), pltpu.SemaphoreType.DMA])
    return pl.pallas_call(
        functools.partial(_combine_kernel, rows=rows, base=base),
        out_shape=jax.ShapeDtypeStruct((B, S, D), F32), grid_spec=grid_spec,
        compiler_params=_cparams(("arbitrary", "arbitrary")), name="moe_combine",
    )(pos, og, wsel, x, gate_f)


def _moe(hs, wsels, isels, xs, gates_f, wg, wu, wd, j):
    D = hs[0].shape[-1]
    n_exp = wg.shape[1]
    h_all = jnp.concatenate([h.reshape(-1, D) for h in hs], axis=0)
    e_flat = jnp.concatenate([i.reshape(-1, LANES)[:, :TOP_K] for i in isels], axis=0).reshape(-1)
    n_assign = e_flat.shape[0]
    n_tiles = -(-n_assign // MOE_TILE) + n_exp
    n_rows = n_tiles * MOE_TILE

    blk = LANES
    n_blk = -(-n_assign // blk)
    onehot = (e_flat[:, None] == jnp.arange(n_exp, dtype=jnp.int32)[None, :]).astype(F32)
    oh3 = jnp.pad(onehot, ((0, n_blk * blk - n_assign), (0, 0))).reshape(n_blk, blk, n_exp)
    tril = (jnp.arange(blk)[:, None] >= jnp.arange(blk)[None, :]).astype(F32)
    within = jnp.einsum("ij,bje->bie", tril, oh3, precision=lax.Precision.HIGHEST)
    blk_tot = within[:, -1]
    blk_off = jnp.cumsum(blk_tot, axis=0) - blk_tot
    csum = (within + blk_off[:, None]).reshape(n_blk * blk, n_exp)[:n_assign]
    rank = jnp.sum(csum * onehot, axis=1).astype(jnp.int32) - 1
    counts = (blk_off[-1] + blk_tot[-1]).astype(jnp.int32)
    gsize = ((counts + MOE_TILE - 1) // MOE_TILE) * MOE_TILE
    gend = jnp.cumsum(gsize)
    gstart = gend - gsize
    pos = (gstart[e_flat] + rank).astype(jnp.int32)
    row_token = jnp.zeros((n_rows,), jnp.int32).at[pos].set(jnp.arange(n_assign, dtype=jnp.int32) // TOP_K)
    n_used = (gend[-1] // MOE_TILE).astype(jnp.int32)
    tile_start = jnp.arange(n_tiles, dtype=jnp.int32) * MOE_TILE
    tile_expert = jnp.sum((tile_start[:, None] >= gend[None, :]).astype(jnp.int32), axis=1)
    last_expert = jnp.sum((((n_used - 1) * MOE_TILE) >= gend).astype(jnp.int32))
    tile_expert = jnp.where(tile_start < gend[-1], tile_expert, last_expert).astype(jnp.int32)
    n_used1 = n_used.reshape(1)

    xg = _gather_rows(h_all, row_token, n_used1, n_rows)
    act = _grouped_up(xg, wg, wu, j, tile_expert, n_used1)
    og = _grouped_down(act, wd, j, tile_expert, n_used1)
    outs, base = [], 0
    for x, ws, gf in zip(xs, wsels, gates_f):
        outs.append(_combine(og, pos, ws, x, gf, base))
        base += x.shape[0] * x.shape[1]
    return outs


def _head_sum(x):
    s = x[:, 0:LANES]
    for i in range(1, MIX_WIDTH // LANES):
        s = s + x[:, i * LANES:(i + 1) * LANES]
    s = s + pltpu.roll(s, 64, axis=1)
    s = s + pltpu.roll(s, 32, axis=1)
    return s


def _tile16(s):
    return jnp.concatenate([s] * (MIX_WIDTH // LANES), axis=1)


def _rwkv_prep_kernel(z_ref, pz_ref, sh0_ref, mu_ref, wup_ref, w0_ref, aup_ref, a0_ref, gup_ref,
                      kk_ref, ka_ref, rk_ref,
                      a_out, wr_out, w_out, b_out, k_out, v_out, g_out, br_out, kr_out, rkr_out):
    s = pl.program_id(1)
    z = z_ref[0]
    ts = z.shape[0]
    prev = jnp.where(s == 0, sh0_ref[0], pz_ref[0][SUBLANES - 1:SUBLANES])
    rid = lax.broadcasted_iota(jnp.int32, z.shape, 0)
    zs = jnp.where(rid == 0, prev, pltpu.roll(z, 1, axis=0))
    zz = z + (zs - z) * mu_ref[0]
    r = zz[:, OFF_R:OFF_R + MIX_WIDTH]
    k = zz[:, OFF_K:OFF_K + MIX_WIDTH]
    v = zz[:, OFF_V:OFF_V + MIX_WIDTH]
    wl = jnp.tanh(zz[:, OFF_LORA:OFF_LORA + LANES])
    al = zz[:, OFF_LORA + LANES:OFF_LORA + 2 * LANES]
    gl = jax.nn.sigmoid(zz[:, OFF_LORA + 2 * LANES:OFF_LORA + 2 * LANES + GATE_LORA])

    w_raw = w0_ref[0] + jnp.dot(wl.astype(BF16), wup_ref[0].astype(BF16), preferred_element_type=F32)
    nx = -w_raw
    softplus = jnp.maximum(nx, 0.0) + jnp.log(1.0 + jnp.exp(-jnp.abs(nx)))
    w = jnp.exp(-jnp.exp(-softplus - 0.5))
    a = jax.nn.sigmoid(a0_ref[0] + jnp.dot(al.astype(BF16), aup_ref[0].astype(BF16),
                                           preferred_element_type=F32))
    g = jnp.dot(gl.astype(BF16), gup_ref[0].astype(BF16), preferred_element_type=F32)

    kk = k * kk_ref[0]
    nrm = jnp.sqrt(_head_sum(kk * kk))
    kk = kk * _tile16(1.0 / jnp.maximum(nrm, 1e-12))
    kt = k * (1.0 + (a - 1.0) * ka_ref[0])
    bs = kk * a

    a_out[0] = -kk
    wr_out[0] = w * r
    w_out[0] = w
    b_out[0] = bs
    k_out[0] = kt
    v_out[0] = v
    g_out[0] = g
    br_out[0] = _head_sum(bs * r)
    kr_out[0] = _head_sum(kt * r)
    rkr_out[0] = _head_sum(r * kt * rk_ref[0])


def _rwkv_prep(z_all, shift0, lw):
    B, S, NP = z_all.shape
    ts = _pick(S, (128, 64, 32, 16, 8))
    grid = (B, S // ts)
    nb8 = ts // SUBLANES
    vec = lambda n: pl.BlockSpec((1, 1, n), lambda b, s: (0, 0, 0))
    mat = lambda r, c: pl.BlockSpec((1, r, c), lambda b, s: (0, 0, 0))
    in_specs = [
        pl.BlockSpec((1, ts, RWKV_PAD), lambda b, s: (b, s, 0)),
        pl.BlockSpec((1, SUBLANES, RWKV_PAD), lambda b, s: (b, jnp.maximum(s * nb8 - 1, 0), 0)),
        pl.BlockSpec((1, 1, RWKV_PAD), lambda b, s: (b, 0, 0)),
        vec(RWKV_PAD),
        mat(LANES, MIX_WIDTH), vec(MIX_WIDTH),
        mat(LANES, MIX_WIDTH), vec(MIX_WIDTH),
        mat(GATE_LORA, MIX_WIDTH),
        vec(MIX_WIDTH), vec(MIX_WIDTH), vec(MIX_WIDTH),
    ]
    big = pl.BlockSpec((1, ts, MIX_WIDTH), lambda b, s: (b, s, 0))
    small = pl.BlockSpec((1, ts, LANES), lambda b, s: (b, s, 0))
    out_shape = [jax.ShapeDtypeStruct((B, S, MIX_WIDTH), F32)] * 7 + \
                [jax.ShapeDtypeStruct((B, S, LANES), F32)] * 3
    return pl.pallas_call(
        _rwkv_prep_kernel, out_shape=out_shape, grid=grid, in_specs=in_specs,
        out_specs=[big] * 7 + [small] * 3,
        compiler_params=_cparams(("parallel", "arbitrary")), name="rwkv_prep",
    )(z_all, z_all, shift0, lw["mu"], lw["w_up"], lw["w0"], lw["a_up"], lw["a0"], lw["g_up"],
      lw["k_k"], lw["k_a"], lw["r_k"])


N_VH = HEAD_DIM // SUBLANES
N_KH = MIX_WIDTH // LANES
N_STATE_TILES = N_VH * N_KH


def _lane_group_sum(x):
    x = x + pltpu.roll(x, 64, axis=1)
    return x + pltpu.roll(x, 32, axis=1)


def _rwkv_scan_kernel(a_ref, wr_ref, w_ref, b_ref, k_ref, v_ref, br_ref, kr_ref, s0_ref,
                      y_ref, st_ref, S, bc, ybuf, *, tc):
    c = pl.program_id(1)

    @pl.when(c == 0)
    def _():
        S[...] = s0_ref[0]

    sub = lax.broadcasted_iota(jnp.int32, (SUBLANES, LANES), 0)
    lane_grp = lax.broadcasted_iota(jnp.int32, (SUBLANES, LANES), 1) >> 5
    diag = lane_grp == (sub & 3)
    low_half = sub < 4
    key_refs = (a_ref, wr_ref, w_ref, b_ref, k_ref)

    def token(base, jj):
        def row(ref, tile):
            return ref[0, pl.ds(base, SUBLANES), tile * LANES:(tile + 1) * LANES][jj:jj + 1]

        bcs = bc.at[jj % 2]
        for i, ref in enumerate(key_refs):
            for kh in range(N_KH):
                bcs[i * N_KH + kh] = jnp.broadcast_to(row(ref, kh), (SUBLANES, LANES))
        brr = row(br_ref, 0)
        krr = row(kr_ref, 0)

        def tree_sum(terms):
            while len(terms) > 1:
                terms = [terms[i] + terms[i + 1] for i in range(0, len(terms), 2)]
            return terms[0]

        for vh in range(N_VH):
            hs = [S[vh * N_KH + kh] for kh in range(N_KH)]
            u = _lane_group_sum(tree_sum([hs[kh] * bcs[kh] for kh in range(N_KH)]))
            o = _lane_group_sum(tree_sum([hs[kh] * bcs[N_KH + kh] for kh in range(N_KH)]))
            vsel = jnp.where(low_half, row(v_ref, 2 * vh), row(v_ref, 2 * vh + 1))
            vv = _lane_group_sum(jnp.where(diag, vsel, 0.0))
            y = o + u * brr + vv * krr
            yd = jnp.where(diag, y, 0.0)
            yd = yd + pltpu.roll(yd, 2, axis=0)
            yd = yd + pltpu.roll(yd, 1, axis=0)
            ybuf[jj:jj + 1, 2 * vh * LANES:(2 * vh + 1) * LANES] = yd[3:4]
            ybuf[jj:jj + 1, (2 * vh + 1) * LANES:(2 * vh + 2) * LANES] = yd[7:8]
            for kh in range(N_KH):
                S[vh * N_KH + kh] = (hs[kh] * bcs[2 * N_KH + kh] + bcs[3 * N_KH + kh] * u
                                     + bcs[4 * N_KH + kh] * vv)

    def group(gi, carry):
        base = pl.multiple_of(gi * SUBLANES, SUBLANES)
        for jj in range(SUBLANES):
            token(base, jj)
        y_ref[0, pl.ds(base, SUBLANES), :] = ybuf[...]
        return carry

    lax.fori_loop(0, tc // SUBLANES, group, 0)

    @pl.when(c == pl.num_programs(1) - 1)
    def _():
        st_ref[0] = S[...]


def _rwkv_scan(a, wr, w, b, k, v, br, kr, s0):
    B, S = a.shape[:2]
    tc = _pick(S, (128, 64, 32, 16, 8))
    grid = (B, S // tc)
    kspec = pl.BlockSpec((1, tc, MIX_WIDTH), lambda b, c: (b, c, 0))
    rspec = pl.BlockSpec((1, tc, LANES), lambda b, c: (b, c, 0))
    sspec = pl.BlockSpec((1, N_STATE_TILES, SUBLANES, LANES), lambda b, c: (b, 0, 0, 0))
    return pl.pallas_call(
        functools.partial(_rwkv_scan_kernel, tc=tc),
        out_shape=[jax.ShapeDtypeStruct((B, S, MIX_WIDTH), F32),
                   jax.ShapeDtypeStruct((B, N_STATE_TILES, SUBLANES, LANES), F32)],
        grid=grid,
        in_specs=[kspec] * 6 + [rspec, rspec, sspec],
        out_specs=[kspec, sspec],
        scratch_shapes=[pltpu.VMEM((N_STATE_TILES, SUBLANES, LANES), F32),
                        pltpu.VMEM((2, 5 * N_KH, SUBLANES, LANES), F32),
                        pltpu.VMEM((SUBLANES, MIX_WIDTH), F32)],
        compiler_params=_cparams(("parallel", "arbitrary")), name="rwkv_scan",
    )(a, wr, w, b, k, v, br, kr, s0)


def _rwkv_post_kernel(y_ref, v_ref, g_ref, rkr_ref, gw_ref, gb_ref, o_ref):
    y = y_ref[0]
    inv_n = 1.0 / HEAD_DIM
    d = y - _tile16(_head_sum(y) * inv_n)
    var = _head_sum(d * d) * inv_n
    yn = d * _tile16(lax.rsqrt(var + GN_EPS)) * gw_ref[0] + gb_ref[0]
    o_ref[0] = ((yn + _tile16(rkr_ref[0]) * v_ref[0]) * g_ref[0]).astype(o_ref.dtype)


def _rwkv_post(y, v, g, rkr, lw):
    B, S, _ = y.shape
    ts = _pick(S, (256, 128, 64, 32, 16, 8))
    big = pl.BlockSpec((1, ts, MIX_WIDTH), lambda b, s: (b, s, 0))
    small = pl.BlockSpec((1, ts, LANES), lambda b, s: (b, s, 0))
    vec = pl.BlockSpec((1, 1, MIX_WIDTH), lambda b, s: (0, 0, 0))
    return pl.pallas_call(
        _rwkv_post_kernel, out_shape=jax.ShapeDtypeStruct((B, S, MIX_WIDTH), BF16),
        grid=(B, S // ts), in_specs=[big, big, big, small, vec, vec], out_specs=big,
        compiler_params=_cparams(("parallel", "parallel")), name="rwkv_post",
    )(y, v, g, rkr, lw["gn_w"], lw["gn_b"])


def _conv_kernel(bg_ref, cg_ref, xin_ref, pcg_ref, pxin_ref, buf_ref, cw_ref, y_ref, st_ref):
    s = pl.program_id(2)
    u = cg_ref[0] * xin_ref[0]
    ts = u.shape[0]
    pu = pcg_ref[0] * pxin_ref[0]
    prev2 = jnp.where(s == 0, buf_ref[0], pu[SUBLANES - 2:SUBLANES])
    rid = lax.broadcasted_iota(jnp.int32, u.shape, 0)
    u1 = jnp.where(rid == 0, prev2[1:2], pltpu.roll(u, 1, axis=0))
    u2 = jnp.where(rid == 0, prev2[0:1], jnp.where(rid == 1, prev2[1:2], pltpu.roll(u, 2, axis=0)))
    cw = cw_ref[...]
    y = cw[0:1] * u2 + cw[1:2] * u1 + cw[2:3] * u
    y_ref[0] = (bg_ref[0] * y).astype(y_ref.dtype)

    @pl.when(s == pl.num_programs(2) - 1)
    def _():
        st_ref[0] = u[ts - (CONV_WIDTH - 1):ts]


def _conv(z_all, buf, conv_w, layer):
    B, S, _ = z_all.shape
    ts = _pick(S, (512, 256, 128, 64, 32, 16, 8))
    nb8 = ts // SUBLANES
    nj = MIX_WIDTH // COL_BLK
    c0 = OFF_CONV // COL_BLK
    blk = lambda off: pl.BlockSpec((1, ts, COL_BLK), lambda b, j, s, off=off: (b, s, off + j))
    pblk = lambda off: pl.BlockSpec(
        (1, SUBLANES, COL_BLK), lambda b, j, s, off=off: (b, jnp.maximum(s * nb8 - 1, 0), off + j))
    st_spec = pl.BlockSpec((1, CONV_WIDTH - 1, COL_BLK), lambda b, j, s: (b, 0, j))
    return pl.pallas_call(
        _conv_kernel,
        out_shape=[jax.ShapeDtypeStruct((B, S, MIX_WIDTH), BF16),
                   jax.ShapeDtypeStruct((B, CONV_WIDTH - 1, MIX_WIDTH), F32)],
        grid=(B, nj, S // ts),
        in_specs=[blk(c0), blk(c0 + nj), blk(c0 + 2 * nj), pblk(c0 + nj), pblk(c0 + 2 * nj), st_spec,
                  pl.BlockSpec((None, CONV_WIDTH, COL_BLK), lambda b, j, s: (layer, 0, j))],
        out_specs=[pl.BlockSpec((1, ts, COL_BLK), lambda b, j, s: (b, s, j)), st_spec],
        compiler_params=_cparams(("parallel", "parallel", "arbitrary")), name="gated_conv",
    )(z_all, z_all, z_all, z_all, z_all, buf, conv_w)


def _attn_core(q_refs, k_all, v_all, sinks_ref, q0, k0, o_ref):
    tq = q_refs[0].shape[1]
    tk = k_all.shape[0]
    rows = GROUP * tq
    tq_shift = tq.bit_length() - 1
    assert tq == 1 << tq_shift
    ri = lax.broadcasted_iota(jnp.int32, (rows, tk), 0)
    qpos = q0 + (ri & (tq - 1))
    kpos = k0 + lax.broadcasted_iota(jnp.int32, (rows, tk), 1)
    dc = (qpos >> CHUNK_SHIFT) - (kpos >> CHUNK_SHIFT)
    valid = (dc >= 0) & (dc <= W_CHUNKS) & (kpos >= 0)
    dist = jnp.abs(qpos - kpos).astype(F32)
    jrow = ri >> tq_shift
    jslope = jnp.where(jrow == 0, 2.0 ** -0.25, jnp.where(jrow == 1, 2.0 ** -0.5,
                                                         jnp.where(jrow == 2, 2.0 ** -0.75, 0.5)))
    bias0 = jnp.where(valid, -jslope * dist, -1e30)
    jcol = lax.broadcasted_iota(jnp.int32, (rows, 1), 0) >> tq_shift
    heads_per_blk = COL_BLK // HEAD_DIM
    for g in range(KV_HEADS):
        kg = k_all[:, g * HEAD_DIM:(g + 1) * HEAD_DIM].astype(BF16)
        vg = v_all[:, g * HEAD_DIM:(g + 1) * HEAD_DIM].astype(BF16)
        qs, sk = [], None
        for j in range(GROUP):
            h = g * GROUP + j
            lo = (h % heads_per_blk) * HEAD_DIM
            qs.append(q_refs[h // heads_per_blk][0, :, lo:lo + HEAD_DIM])
            sk = sinks_ref[h] if sk is None else jnp.where(jcol == j, sinks_ref[h], sk)
        qg = jnp.concatenate(qs, axis=0).astype(BF16)
        s = lax.dot_general(qg, kg, (((1,), (1,)), ((), ())), preferred_element_type=F32)
        s = s * ATTN_SCALE + bias0 * (2.0 ** -g)
        m = jnp.maximum(jnp.max(s, axis=-1, keepdims=True), sk)
        p = jnp.exp(s - m)
        den = jnp.sum(p, axis=-1, keepdims=True) + jnp.exp(sk - m)
        o = jnp.dot(p.astype(BF16), vg, preferred_element_type=F32) / den
        for j in range(GROUP):
            h = g * GROUP + j
            o_ref[0, :, h * HEAD_DIM:(h + 1) * HEAD_DIM] = o[j * tq:(j + 1) * tq].astype(o_ref.dtype)


def _attn_prompt_kernel(sinks_ref, q0_ref, q1_ref, q2_ref, q3_ref, ka_ref, kb_ref, kc_ref,
                        va_ref, vb_ref, vc_ref, o_ref):
    n = pl.program_id(1)
    k_all = jnp.concatenate([ka_ref[0], kb_ref[0], kc_ref[0]], axis=0)
    v_all = jnp.concatenate([va_ref[0], vb_ref[0], vc_ref[0]], axis=0)
    _attn_core((q0_ref, q1_ref, q2_ref, q3_ref), k_all, v_all, sinks_ref,
               n * CHUNK, (n - W_CHUNKS) * CHUNK, o_ref)


def _attn_prompt(z_all, sinks):
    B, S, _ = z_all.shape
    qb = OFF_ATTN // COL_BLK
    kb = qb + MIX_WIDTH // COL_BLK
    vb = kb + 1
    qspec = lambda i: pl.BlockSpec((1, CHUNK, COL_BLK), lambda b, n, s, i=i: (b, n, qb + i))
    wspec = lambda back, col: pl.BlockSpec(
        (1, CHUNK, COL_BLK), lambda b, n, s, back=back, col=col: (b, jnp.maximum(n - back, 0), col))
    grid_spec = pltpu.PrefetchScalarGridSpec(
        num_scalar_prefetch=1, grid=(B, S // CHUNK),
        in_specs=[qspec(0), qspec(1), qspec(2), qspec(3),
                  wspec(2, kb), wspec(1, kb), wspec(0, kb), wspec(2, vb), wspec(1, vb), wspec(0, vb)],
        out_specs=pl.BlockSpec((1, CHUNK, MIX_WIDTH), lambda b, n, s: (b, n, 0)))
    return pl.pallas_call(
        _attn_prompt_kernel, out_shape=jax.ShapeDtypeStruct((B, S, MIX_WIDTH), BF16),
        grid_spec=grid_spec, compiler_params=_cparams(("parallel", "parallel")), name="swa_prompt",
    )(sinks, *([z_all] * 10))


def _attn_sample_kernel(sinks_ref, q0_ref, q1_ref, q2_ref, q3_ref, kc_ref, kn_ref, vc_ref, vn_ref,
                        o_ref, *, pos0):
    L = kc_ref.shape[1]
    k_all = jnp.concatenate([kc_ref[0], kn_ref[0]], axis=0)
    v_all = jnp.concatenate([vc_ref[0], vn_ref[0]], axis=0)
    _attn_core((q0_ref, q1_ref, q2_ref, q3_ref), k_all, v_all, sinks_ref, pos0, pos0 - L, o_ref)


def _attn_sample(z_all, k_cache, v_cache, sinks, pos0):
    B, S, _ = z_all.shape
    L = k_cache.shape[1]
    qb = OFF_ATTN // COL_BLK
    kb = qb + MIX_WIDTH // COL_BLK
    vb = kb + 1
    qspec = lambda i: pl.BlockSpec((1, S, COL_BLK), lambda b, s, i=i: (b, 0, qb + i))
    nspec = lambda col: pl.BlockSpec((1, S, COL_BLK), lambda b, s, col=col: (b, 0, col))
    cspec = pl.BlockSpec((1, L, KV_WIDTH), lambda b, s: (b, 0, 0))
    grid_spec = pltpu.PrefetchScalarGridSpec(
        num_scalar_prefetch=1, grid=(B,),
        in_specs=[qspec(0), qspec(1), qspec(2), qspec(3), cspec, nspec(kb), cspec, nspec(vb)],
        out_specs=pl.BlockSpec((1, S, MIX_WIDTH), lambda b, s: (b, 0, 0)))
    return pl.pallas_call(
        functools.partial(_attn_sample_kernel, pos0=pos0),
        out_shape=jax.ShapeDtypeStruct((B, S, MIX_WIDTH), BF16),
        grid_spec=grid_spec, compiler_params=_cparams(("parallel",)), name="swa_sample",
    )(sinks, z_all, z_all, z_all, z_all, k_cache, z_all, v_cache, z_all)


def _merge_kernel(ya_ref, yb_ref, yc_ref, wa_ref, wb_ref, wc_ref, ga_ref, gb_ref, gc_ref, o_ref, *, tm):
    acc = None
    for y_ref, w_ref, g_ref in ((ya_ref, wa_ref, ga_ref), (yb_ref, wb_ref, gb_ref), (yc_ref, wc_ref, gc_ref)):
        y = y_ref[...].reshape(tm, y_ref.shape[-1])
        gate = jax.nn.sigmoid(g_ref[...].reshape(tm, g_ref.shape[-1]))
        c = gate * jnp.dot(y, w_ref[...].astype(BF16), preferred_element_type=F32)
        acc = c if acc is None else acc + c
    o_ref[...] = acc.reshape(o_ref.shape).astype(o_ref.dtype)


def _merge(ya, yb, yc, w0p, w_branch_out, layer, z_all, D):
    B, S, _ = ya.shape
    bb, ts = _row_tiles(B, S, 1024)
    tm = bb * ts
    tn = 256
    g0 = OFF_GATE // tn
    nd = D // tn
    yspec = pl.BlockSpec((bb, ts, MIX_WIDTH), lambda b, m, n: (b, m, 0))
    wspec = lambda r: pl.BlockSpec((None, None, MIX_WIDTH, tn), lambda b, m, n, r=r: (layer, r, 0, n))
    gspec = lambda r: pl.BlockSpec((bb, ts, tn), lambda b, m, n, r=r: (b, m, g0 + r * nd + n))
    return pl.pallas_call(
        functools.partial(_merge_kernel, tm=tm),
        out_shape=jax.ShapeDtypeStruct((B, S, D), BF16),
        grid=(B // bb, S // ts, nd),
        in_specs=[yspec, yspec, yspec,
                  pl.BlockSpec((MIX_WIDTH, tn), lambda b, m, n: (0, n)), wspec(1), wspec(2),
                  gspec(0), gspec(1), gspec(2)],
        out_specs=pl.BlockSpec((bb, ts, tn), lambda b, m, n: (b, m, n)),
        compiler_params=_cparams(("parallel", "parallel", "parallel")), name="branch_merge",
    )(ya, yb, yc, w0p, w_branch_out, w_branch_out, z_all, z_all, z_all)


def _im_cols(w):
    lead = w.shape[:-1]
    return jnp.swapaxes(w.reshape(lead + (N_HEADS, HEAD_DIM)), -1, -2).reshape(lead + (MIX_WIDTH,))


def _im_cols_inv(w):
    lead = w.shape[:-1]
    return jnp.swapaxes(w.reshape(lead + (HEAD_DIM, N_HEADS)), -1, -2).reshape(lead + (MIX_WIDTH,))


def _pack_rwkv_cols(t):
    lead = t.shape[:-1]
    z32 = jnp.zeros(lead + (LANES - DECAY_LORA,), t.dtype)
    o = 3 * MIX_WIDTH
    return jnp.concatenate([
        _im_cols(t[..., 0:MIX_WIDTH]), _im_cols(t[..., MIX_WIDTH:2 * MIX_WIDTH]),
        _im_cols(t[..., 2 * MIX_WIDTH:3 * MIX_WIDTH]),
        t[..., o:o + DECAY_LORA], z32,
        t[..., o + DECAY_LORA:o + DECAY_LORA + ICLR_LORA], z32,
        t[..., o + DECAY_LORA + ICLR_LORA:]], axis=-1)


def _unpack_rwkv_cols(t):
    o = OFF_LORA
    return jnp.concatenate([
        _im_cols_inv(t[..., 0:MIX_WIDTH]), _im_cols_inv(t[..., MIX_WIDTH:2 * MIX_WIDTH]),
        _im_cols_inv(t[..., 2 * MIX_WIDTH:3 * MIX_WIDTH]),
        t[..., o:o + DECAY_LORA], t[..., o + LANES:o + LANES + ICLR_LORA],
        t[..., o + 2 * LANES:o + 2 * LANES + GATE_LORA]], axis=-1)


def _pad_rows(w, rows):
    return jnp.concatenate([w, jnp.zeros((rows - w.shape[0],) + w.shape[1:], w.dtype)], axis=0)


def _state_to_tiles(s):
    B = s.shape[0]
    s = s.reshape(B, N_HEADS, N_VH, SUBLANES, N_KH, 4)
    s = jnp.transpose(s, (0, 2, 4, 3, 5, 1))
    return s.reshape(B, N_STATE_TILES, SUBLANES, LANES)


def _tiles_to_state(s):
    B = s.shape[0]
    s = s.reshape(B, N_VH, N_KH, SUBLANES, 4, N_HEADS)
    s = jnp.transpose(s, (0, 5, 1, 3, 2, 4))
    return s.reshape(B, N_HEADS, HEAD_DIM, HEAD_DIM)


def _mixers(i, x, m, st, pos0, lw, P):
    B, S, D = x.shape
    h = _norm(x, P["norm_mix_g"], i, m[1], m[0], name="norm_mix")
    z_all = _mm(h, [lw["w_all"]], [()], tm_max=1024, tn=512, tk=D, name="in_proj")

    if st is None:
        s0 = jnp.zeros((B, N_STATE_TILES, SUBLANES, LANES), F32)
        shift0 = jnp.zeros((B, 1, RWKV_PAD), F32)
        conv0 = jnp.zeros((B, CONV_WIDTH - 1, MIX_WIDTH), F32)
    else:
        s0 = _state_to_tiles(st[0].astype(F32))
        shift0 = _pack_rwkv_cols(st[1].astype(F32))
        conv0 = st[2].astype(F32)

    a, wr, w, b, k, v, g, br, kr, rkr = _rwkv_prep(z_all, shift0, lw)
    y, s_tiles = _rwkv_scan(a, wr, w, b, k, v, br, kr, s0)
    y_a = _rwkv_post(y, v, g, rkr, lw)
    s_new = _tiles_to_state(s_tiles)
    shift_new = _unpack_rwkv_cols(z_all[:, S - 1:S, 0:RWKV_PAD])

    y_b, conv_new = _conv(z_all, conv0, P["conv_w"], i)

    k_new_all = z_all[:, :, OFF_ATTN + MIX_WIDTH:OFF_ATTN + MIX_WIDTH + KV_WIDTH]
    v_new_all = z_all[:, :, OFF_ATTN + MIX_WIDTH + KV_WIDTH:OFF_ATTN + MIX_WIDTH + 2 * KV_WIDTH]
    if st is None:
        y_c = _attn_prompt(z_all, lw["sinks"])
        k_new = k_new_all[:, S - WINDOW:].reshape(B, WINDOW, KV_HEADS, HEAD_DIM)
        v_new = v_new_all[:, S - WINDOW:].reshape(B, WINDOW, KV_HEADS, HEAD_DIM)
    else:
        L = st[3].shape[1]
        kc = st[3].astype(F32).reshape(B, L, KV_WIDTH)
        vc = st[4].astype(F32).reshape(B, L, KV_WIDTH)
        y_c = _attn_sample(z_all, kc, vc, lw["sinks"], pos0)
        k_new = jnp.concatenate([kc, k_new_all], axis=1)[:, -L:].reshape(B, L, KV_HEADS, HEAD_DIM)
        v_new = jnp.concatenate([vc, v_new_all], axis=1)[:, -L:].reshape(B, L, KV_HEADS, HEAD_DIM)

    merged = _merge(y_a, y_b, y_c, lw["wb0"], P["w_branch_out"], i, z_all, D)
    x = _mm(merged, [P["w_mix_out"]], [(i,)], tm_max=1024, tn=512, tk=D, epi="resgate",
            extras=(x, m[2]), name="mix_out")
    return x, (s_new, shift_new, conv_new, k_new, v_new)


def _channel_mixers(i, xs, ms, lw, P):
    j = i // 2
    if i % 2 == 0:
        outs = []
        for x, m in zip(xs, ms):
            h = _norm(x, P["norm_ffn_g"], i, m[4], m[3], name="norm_ffn")
            act = _mm(h, [P["ffn_w_gate"], P["ffn_w_up"]], [(j,), (j,)], tm_max=2048, tn=256,
                      tk=h.shape[-1], epi="swiglu", out_dtype=BF16, name="ffn_up")
            outs.append(_mm(act, [P["ffn_w_down"]], [(j,)], tm_max=1024, tn=1024, tk=2048, epi="resgate",
                            extras=(x, m[5]), name="ffn_down"))
        return outs
    n_exp = P["moe_w_gate"].shape[1]
    hs, wsels, isels = [], [], []
    for x, m in zip(xs, ms):
        h, logits = _norm(x, P["norm_ffn_g"], i, m[4], m[3], router=lw["router"], out_dtype=F32,
                          name="norm_moe")
        wsel, isel = _topk_route(logits, n_exp)
        hs.append(h)
        wsels.append(wsel)
        isels.append(isel)
    return _moe(hs, wsels, isels, xs, [m[5] for m in ms], P["moe_w_gate"], P["moe_w_up"],
                P["moe_w_down"], j)


def kernel(x_prompt, x_sample, c_prompt, c_sample, state_rwkv, state_rwkv_shift, state_conv, cache_swa_k, cache_swa_v, ada_w, ada_b, norm_mix_g, norm_ffn_g, w_in, rwkv_mu, rwkv_w_up, rwkv_w0, rwkv_a_up, rwkv_a0, rwkv_g_up, rwkv_k_k, rwkv_k_a, rwkv_r_k, rwkv_gn_w, rwkv_gn_b, conv_w, attn_sinks, w_branch_out, w_mix_out, ffn_w_gate, ffn_w_up, ffn_w_down, moe_router, moe_w_gate, moe_w_up, moe_w_down, final_norm_g):
    depth, D = norm_mix_g.shape
    Bp, Sp, _ = x_prompt.shape
    Bs, Ss, _ = x_sample.shape

    P = dict(norm_mix_g=norm_mix_g.reshape(depth, 1, D), norm_ffn_g=norm_ffn_g.reshape(depth, 1, D),
             conv_w=conv_w, w_branch_out=w_branch_out, w_mix_out=w_mix_out,
             ffn_w_gate=ffn_w_gate, ffn_w_up=ffn_w_up, ffn_w_down=ffn_w_down,
             moe_w_gate=moe_w_gate, moe_w_up=moe_w_up, moe_w_down=moe_w_down)

    nc = Bp + Bs
    nc_pad = -(-nc // 16) * 16
    c_all = jnp.concatenate([c_prompt, c_sample, jnp.zeros((nc_pad - nc, D), F32)], axis=0)[None]
    ada_b3 = ada_b.reshape(depth, 1, 6 * D)

    xp, xs = x_prompt, x_sample
    outs_p, outs_s = [], []
    for i in range(depth):
        mod = _mm(c_all, [ada_w], [(i,)], tm_max=2048, tn=1024, tk=2048, epi="bias", extras=(ada_b3, i),
                  silu_in=True, name="ada_mod")[0]
        mod_p = mod[:Bp].reshape(Bp, 6, D)
        mod_s = mod[Bp:nc].reshape(Bs, 6, D)
        ms_p = [mod_p[:, j:j + 1] for j in range(6)]
        ms_s = [mod_s[:, j:j + 1] for j in range(6)]

        o = 3 * MIX_WIDTH
        wi = w_in[i]
        zc = jnp.zeros((D, LANES - DECAY_LORA), BF16)
        cb = lambda t: t.astype(BF16)
        w_all = jnp.concatenate([
            cb(_im_cols(wi[:, 0:MIX_WIDTH])), cb(_im_cols(wi[:, MIX_WIDTH:2 * MIX_WIDTH])),
            cb(_im_cols(wi[:, 2 * MIX_WIDTH:3 * MIX_WIDTH])),
            cb(wi[:, o:o + DECAY_LORA]), zc, cb(wi[:, o + DECAY_LORA:o + DECAY_LORA + ICLR_LORA]), zc,
            cb(wi[:, o + DECAY_LORA + ICLR_LORA:])], axis=1)
        vec = lambda t: _im_cols(t[i]).reshape(1, 1, MIX_WIDTH)
        lw = dict(
            w_all=w_all,
            mu=_pack_rwkv_cols(rwkv_mu[i]).reshape(1, 1, RWKV_PAD),
            w_up=_pad_rows(_im_cols(rwkv_w_up[i]), LANES)[None], w0=vec(rwkv_w0),
            a_up=_pad_rows(_im_cols(rwkv_a_up[i]), LANES)[None], a0=vec(rwkv_a0),
            g_up=_im_cols(rwkv_g_up[i])[None],
            k_k=vec(rwkv_k_k), k_a=vec(rwkv_k_a), r_k=vec(rwkv_r_k),
            gn_w=vec(rwkv_gn_w), gn_b=vec(rwkv_gn_b),
            sinks=attn_sinks[i],
            wb0=jnp.swapaxes(w_branch_out[i, 0].reshape(N_HEADS, HEAD_DIM, D), 0, 1).reshape(MIX_WIDTH, D),
        )
        if i % 2 == 1:
            r = moe_router[i // 2]
            lw["router"] = jnp.concatenate([r, jnp.zeros((D, LANES - r.shape[1]), F32)], axis=1)

        xp, st_p = _mixers(i, xp, ms_p, None, 0, lw, P)
        st_s_in = (state_rwkv[i], state_rwkv_shift[i], state_conv[i], cache_swa_k[i], cache_swa_v[i])
        xs, st_s = _mixers(i, xs, ms_s, st_s_in, PAST_LEN, lw, P)
        xp, xs = _channel_mixers(i, [xp, xs], [ms_p, ms_s], lw, P)
        outs_p.append(st_p)
        outs_s.append(st_s)

    fg = final_norm_g.reshape(1, 1, D)
    y_prompt = _norm(xp, fg, 0, out_dtype=F32, name="norm_final")
    y_sample = _norm(xs, fg, 0, out_dtype=F32, name="norm_final")
    stack = lambda outs, idx: jnp.stack([o[idx] for o in outs])
    return (y_prompt, y_sample,
            stack(outs_p, 0), stack(outs_p, 1), stack(outs_p, 2), stack(outs_p, 3), stack(outs_p, 4),
            stack(outs_s, 0), stack(outs_s, 1), stack(outs_s, 2), stack(outs_s, 3), stack(outs_s, 4))
```

```python
import functools
import math

import jax
import jax.numpy as jnp
from jax import lax
from jax.experimental import pallas as pl
from jax.experimental.pallas import tpu as pltpu

F32 = jnp.float32
BF16 = jnp.bfloat16

HEAD_DIM = 64
N_HEADS = 32
MIX_WIDTH = N_HEADS * HEAD_DIM
KV_HEADS = 8
GROUP = N_HEADS // KV_HEADS
KV_WIDTH = KV_HEADS * HEAD_DIM
CHUNK = 64
CHUNK_SHIFT = 6
W_CHUNKS = 2
WINDOW = W_CHUNKS * CHUNK
PAST_LEN = 1024
CONV_WIDTH = 3
DECAY_LORA = 96
ICLR_LORA = 96
GATE_LORA = 256
N_BRANCH = 3
TOP_K = 2
RMS_EPS = 1e-5
GN_EPS = 64e-5
ATTN_SCALE = HEAD_DIM ** -0.5

LANES = 128
SUBLANES = 8
VMEM_LIMIT_BYTES = 56 * 1024 * 1024

LORA_PAD = 512
OFF_R, OFF_K, OFF_V = 0, MIX_WIDTH, 2 * MIX_WIDTH
OFF_LORA = 3 * MIX_WIDTH
RWKV_PAD = OFF_LORA + LORA_PAD
OFF_CONV = RWKV_PAD
OFF_ATTN = OFF_CONV + 3 * MIX_WIDTH
OFF_GATE = OFF_ATTN + MIX_WIDTH + 2 * KV_WIDTH
COL_BLK = 512
RWKV_COLS = 3 * MIX_WIDTH + DECAY_LORA + ICLR_LORA + GATE_LORA


def _cparams(sem):
    return pltpu.CompilerParams(dimension_semantics=sem, vmem_limit_bytes=VMEM_LIMIT_BYTES)


def _pick(n, cands):
    for c in cands:
        if n % c == 0:
            return c
    return n


def _row_tiles(B, S, tm_max):
    if S >= tm_max:
        ts = _pick(S, (tm_max, tm_max // 2, tm_max // 4, 256, 128, 64, 32, 16, 8))
        return 1, ts
    bb = max(1, min(B, tm_max // S))
    while B % bb:
        bb -= 1
    return bb, S


def _mm_kernel(*refs, n_w, epi, tm, nk, silu_in):
    x_ref = refs[0]
    w_refs = refs[1:1 + n_w]
    pos = 1 + n_w
    n_extra = {"none": 0, "swiglu": 0, "bias": 1, "resgate": 2}[epi]
    extra = refs[pos:pos + n_extra]
    pos += n_extra
    o_ref = refs[pos]
    acc_refs = refs[pos + 1:]

    x = x_ref[...]
    x = x.reshape(tm, x.shape[-1])
    if silu_in:
        xf = x.astype(F32)
        x = xf * jax.nn.sigmoid(xf)
    x = x.astype(BF16)

    def finish(accs):
        a = accs[0]
        if epi == "swiglu":
            r = (a * jax.nn.sigmoid(a)) * accs[1]
        elif epi == "bias":
            r = a + extra[0][...]
        elif epi == "resgate":
            r = extra[0][...] + extra[1][...] * a.reshape(o_ref.shape)
        else:
            r = a
        o_ref[...] = r.reshape(o_ref.shape).astype(o_ref.dtype)

    dots = [jnp.dot(x, w_refs[j][...].astype(BF16), preferred_element_type=F32) for j in range(n_w)]
    if nk == 1:
        finish(dots)
        return

    k = pl.program_id(3)
    if not acc_refs:
        acc_refs = (o_ref,)

    @pl.when(k == 0)
    def _():
        for acc in acc_refs:
            acc[...] = jnp.zeros_like(acc)

    for j in range(n_w):
        acc_refs[j][...] += dots[j].reshape(acc_refs[j].shape)

    if epi != "none" or acc_refs[0] is not o_ref:
        @pl.when(k == nk - 1)
        def _():
            finish([acc[...] for acc in acc_refs])


def _mm(x, ws, leads, *, tm_max, tn, tk, epi="none", out_dtype=F32, extras=(), silu_in=False,
        name="mm"):
    B, S, K = x.shape
    N = ws[0].shape[-1]
    bb, ts = _row_tiles(B, S, tm_max)
    tm = bb * ts
    tn = _pick(N, (tn, 512, 256, 128))
    tk = _pick(K, (tk, 1024, 512, 256, 128))
    nk = K // tk
    grid = (B // bb, S // ts, N // tn, nk)

    in_specs = [pl.BlockSpec((bb, ts, tk), lambda b, m, n, k: (b, m, k))]
    for w, lead in zip(ws, leads):
        nl = len(lead)
        in_specs.append(pl.BlockSpec((None,) * nl + (tk, tn),
                                     lambda b, m, n, k, lead=lead: tuple(lead) + (k, n)))
    if epi == "bias":
        in_specs.append(pl.BlockSpec((None, 1, tn), lambda b, m, n, k, l=extras[1]: (l, 0, n)))
        extras = extras[:1]
    elif epi == "resgate":
        in_specs.append(pl.BlockSpec((bb, ts, tn), lambda b, m, n, k: (b, m, n)))
        in_specs.append(pl.BlockSpec((bb, 1, tn), lambda b, m, n, k: (b, 0, n)))
    n_w = len(ws)
    direct = nk == 1 or (n_w == 1 and out_dtype == F32)
    kern = functools.partial(_mm_kernel, n_w=n_w, epi=epi, tm=tm, nk=nk, silu_in=silu_in)
    return pl.pallas_call(
        kern,
        out_shape=jax.ShapeDtypeStruct((B, S, N), out_dtype),
        grid=grid,
        in_specs=in_specs,
        out_specs=pl.BlockSpec((bb, ts, tn), lambda b, m, n, k: (b, m, n)),
        scratch_shapes=[] if direct else [pltpu.VMEM((tm, tn), F32) for _ in range(n_w)],
        compiler_params=_cparams(("parallel", "parallel", "parallel", "arbitrary")),
        name=name,
    )(x, *ws, *extras)


def _norm_kernel(*refs, modulate, route):
    x_ref, g_ref = refs[0], refs[1]
    pos = 2
    if modulate:
        sc_ref, sh_ref = refs[2], refs[3]
        pos = 4
    if route:
        r_ref = refs[pos]
        pos += 1
    o_ref = refs[pos]
    x = x_ref[...]
    ms = jnp.mean(x * x, axis=-1, keepdims=True)
    y = x * lax.rsqrt(ms + RMS_EPS) * g_ref[...]
    if modulate:
        y = y * (1.0 + sc_ref[...]) + sh_ref[...]
    o_ref[...] = y.astype(o_ref.dtype)
    if route:
        l_ref = refs[pos + 1]
        y2 = y.reshape(-1, y.shape[-1])
        logits = jnp.dot(y2, r_ref[...], preferred_element_type=F32, precision=lax.Precision.HIGHEST)
        l_ref[...] = logits.reshape(l_ref.shape)


def _norm(x, g3, lead, scale=None, shift=None, router=None, out_dtype=BF16, name="norm"):
    B, S, D = x.shape
    bb, ts = _row_tiles(B, S, 256)
    grid = (B // bb, S // ts)
    modulate = scale is not None
    route = router is not None
    in_specs = [pl.BlockSpec((bb, ts, D), lambda b, m: (b, m, 0)),
                pl.BlockSpec((None, 1, D), lambda b, m: (lead, 0, 0))]
    args = [x, g3]
    if modulate:
        in_specs += [pl.BlockSpec((bb, 1, D), lambda b, m: (b, 0, 0))] * 2
        args += [scale, shift]
    out_shape = [jax.ShapeDtypeStruct((B, S, D), out_dtype)]
    out_specs = [pl.BlockSpec((bb, ts, D), lambda b, m: (b, m, 0))]
    if route:
        in_specs.append(pl.BlockSpec((D, LANES), lambda b, m: (0, 0)))
        args.append(router)
        out_shape.append(jax.ShapeDtypeStruct((B, S, LANES), F32))
        out_specs.append(pl.BlockSpec((bb, ts, LANES), lambda b, m: (b, m, 0)))
    res = pl.pallas_call(
        functools.partial(_norm_kernel, modulate=modulate, route=route),
        out_shape=out_shape, grid=grid, in_specs=in_specs, out_specs=out_specs,
        compiler_params=_cparams(("parallel", "parallel")), name=name,
    )(*args)
    return res if route else res[0]


def _topk_kernel(l_ref, w_ref, i_ref, *, n_exp):
    l = l_ref[...]
    lane = lax.broadcasted_iota(jnp.int32, l.shape, l.ndim - 1)
    neg = jnp.float32(-jnp.inf)
    l = jnp.where(lane < n_exp, l, neg)
    m1 = jnp.max(l, axis=-1, keepdims=True)
    i1 = jnp.min(jnp.where(l == m1, lane, LANES), axis=-1, keepdims=True)
    l2 = jnp.where(lane == i1, neg, l)
    m2 = jnp.max(l2, axis=-1, keepdims=True)
    i2 = jnp.min(jnp.where(l2 == m2, lane, LANES), axis=-1, keepdims=True)
    e = jnp.exp(m2 - m1)
    den = 1.0 + e
    w_ref[...] = jnp.where(lane == 0, 1.0 / den, jnp.where(lane == 1, e / den, 0.0))
    i_ref[...] = jnp.where(lane == 0, i1, jnp.where(lane == 1, i2, 0))


def _topk_route(logits, n_exp):
    B, S, _ = logits.shape
    bb, ts = _row_tiles(B, S, 1024)
    spec = pl.BlockSpec((bb, ts, LANES), lambda b, m: (b, m, 0))
    return pl.pallas_call(
        functools.partial(_topk_kernel, n_exp=n_exp),
        out_shape=[jax.ShapeDtypeStruct(logits.shape, F32), jax.ShapeDtypeStruct(logits.shape, jnp.int32)],
        grid=(B // bb, S // ts), in_specs=[spec], out_specs=[spec, spec],
        compiler_params=_cparams(("parallel", "parallel")), name="topk_route",
    )(logits)


MOE_TILE = 1024
GATHER_ROWS = 256


def _row_copy(src_hbm, dst_vmem, sem, src_row, dst_row):
    return pltpu.make_async_copy(src_hbm.at[pl.ds(src_row, 1), :], dst_vmem.at[pl.ds(dst_row, 1), :], sem)


def _gather_kernel(tok_ref, nu_ref, h_hbm, o_ref, buf, sem, *, rows):
    base = pl.program_id(0) * rows

    @pl.when(base < nu_ref[0] * MOE_TILE)
    def _():
        def start(r, c):
            _row_copy(h_hbm, buf, sem, tok_ref[base + r], r).start()
            return c

        def wait(r, c):
            _row_copy(h_hbm, buf, sem, 0, r).wait()
            return c

        lax.fori_loop(0, rows, start, 0, unroll=8)
        lax.fori_loop(0, rows, wait, 0, unroll=8)
        o_ref[...] = buf[...].astype(o_ref.dtype)

    @pl.when(base >= nu_ref[0] * MOE_TILE)
    def _():
        o_ref[...] = jnp.zeros_like(o_ref)


def _gather_rows(h, row_token, n_used, n_rows):
    T, D = h.shape
    rows = GATHER_ROWS
    grid_spec = pltpu.PrefetchScalarGridSpec(
        num_scalar_prefetch=2, grid=(n_rows // rows,),
        in_specs=[pl.BlockSpec(memory_space=pl.ANY)],
        out_specs=pl.BlockSpec((rows, D), lambda i, tok, nu: (i, 0)),
        scratch_shapes=[pltpu.VMEM((rows, D), F32), pltpu.SemaphoreType.DMA])
    return pl.pallas_call(
        functools.partial(_gather_kernel, rows=rows),
        out_shape=jax.ShapeDtypeStruct((n_rows, D), BF16), grid_spec=grid_spec,
        compiler_params=_cparams(("arbitrary",)), name="moe_gather",
    )(row_token, n_used, h)


def _gup_kernel(te_ref, nu_ref, x_ref, wg_ref, wu_ref, o_ref):
    m = pl.program_id(0)

    @pl.when(m < nu_ref[0])
    def _():
        x = x_ref[...]
        g = jnp.dot(x, wg_ref[...].astype(BF16), preferred_element_type=F32)
        u = jnp.dot(x, wu_ref[...].astype(BF16), preferred_element_type=F32)
        o_ref[...] = ((g * jax.nn.sigmoid(g)) * u).astype(o_ref.dtype)

    @pl.when(m >= nu_ref[0])
    def _():
        o_ref[...] = jnp.zeros_like(o_ref)


def _grouped_up(xg, wg, wu, j, tile_expert, n_used):
    P, D = xg.shape
    F = wg.shape[-1]
    tm = MOE_TILE
    tn = _pick(F, (256, 128))
    nn = F // tn

    def wmap(m, n, te, nu):
        return (j, te[m], 0, jnp.where(m < nu[0], n, nn - 1))

    grid_spec = pltpu.PrefetchScalarGridSpec(
        num_scalar_prefetch=2, grid=(P // tm, nn),
        in_specs=[pl.BlockSpec((tm, D), lambda m, n, te, nu: (jnp.minimum(m, nu[0] - 1), 0)),
                  pl.BlockSpec((None, None, D, tn), wmap), pl.BlockSpec((None, None, D, tn), wmap)],
        out_specs=pl.BlockSpec((tm, tn), lambda m, n, te, nu: (m, n)))
    return pl.pallas_call(
        _gup_kernel, out_shape=jax.ShapeDtypeStruct((P, F), BF16), grid_spec=grid_spec,
        compiler_params=_cparams(("arbitrary", "arbitrary")), name="moe_up",
    )(tile_expert, n_used, xg, wg, wu)


def _gdown_kernel(te_ref, nu_ref, x_ref, w_ref, o_ref):
    m = pl.program_id(0)
    k = pl.program_id(2)

    @pl.when(k == 0)
    def _():
        o_ref[...] = jnp.zeros_like(o_ref)

    @pl.when(m < nu_ref[0])
    def _():
        o_ref[...] += jnp.dot(x_ref[...], w_ref[...].astype(BF16), preferred_element_type=F32)


def _grouped_down(act, wd, j, tile_expert, n_used):
    P, F = act.shape
    D = wd.shape[-1]
    tm = MOE_TILE
    tn = _pick(D, (2048, 1024, 512, 256, 128))
    tk = _pick(F, (1024, 512, 256, 128))
    nn, nk = D // tn, F // tk

    def live(m, nu, idx, last):
        return jnp.where(m < nu[0], idx, last)

    grid_spec = pltpu.PrefetchScalarGridSpec(
        num_scalar_prefetch=2, grid=(P // tm, nn, nk),
        in_specs=[pl.BlockSpec((tm, tk), lambda m, n, k, te, nu: (jnp.minimum(m, nu[0] - 1), live(m, nu, k, nk - 1))),
                  pl.BlockSpec((None, None, tk, tn),
                               lambda m, n, k, te, nu: (j, te[m], live(m, nu, k, nk - 1), live(m, nu, n, nn - 1)))],
        out_specs=pl.BlockSpec((tm, tn), lambda m, n, k, te, nu: (m, n)))
    return pl.pallas_call(
        _gdown_kernel, out_shape=jax.ShapeDtypeStruct((P, D), F32), grid_spec=grid_spec,
        compiler_params=_cparams(("arbitrary", "arbitrary", "arbitrary")), name="moe_down",
    )(tile_expert, n_used, act, wd)


def _combine_kernel(pos_ref, og_hbm, w_ref, x_ref, g_ref, o_ref, buf, sem, *, rows, base):
    t0 = base + (pl.program_id(0) * pl.num_programs(1) + pl.program_id(1)) * rows

    def start(r, c):
        for jj in range(TOP_K):
            _row_copy(og_hbm, buf.at[jj], sem, pos_ref[TOP_K * (t0 + r) + jj], r).start()
        return c

    def wait(r, c):
        for jj in range(TOP_K):
            _row_copy(og_hbm, buf.at[jj], sem, 0, r).wait()
        return c

    lax.fori_loop(0, rows, start, 0)
    lax.fori_loop(0, rows, wait, 0)
    w = w_ref[0]
    f = w[:, 0:1] * buf[0] + w[:, 1:2] * buf[1]
    o_ref[0] = x_ref[0] + g_ref[0] * f


def _combine(og, pos, wsel, x, gate_f, base):
    B, S, D = x.shape
    rows = _pick(S, (256, 128, 64, 32, 16, 8))
    grid_spec = pltpu.PrefetchScalarGridSpec(
        num_scalar_prefetch=1, grid=(B, S // rows),
        in_specs=[pl.BlockSpec(memory_space=pl.ANY),
                  pl.BlockSpec((1, rows, LANES), lambda b, s, p: (b, s, 0)),
                  pl.BlockSpec((1, rows, D), lambda b, s, p: (b, s, 0)),
                  pl.BlockSpec((1, 1, D), lambda b, s, p: (b, 0, 0))],
        out_specs=pl.BlockSpec((1, rows, D), lambda b, s, p: (b, s, 0)),
        scratch_shapes=[pltpu.VMEM((TOP_K, rows, D), F32), pltpu.SemaphoreType.DMA])
    return pl.pallas_call(
        functools.partial(_combine_kernel, rows=rows, base=base),
        out_shape=jax.ShapeDtypeStruct((B, S, D), F32), grid_spec=grid_spec,
        compiler_params=_cparams(("arbitrary", "arbitrary")), name="moe_combine",
    )(pos, og, wsel, x, gate_f)


def _moe(hs, wsels, isels, xs, gates_f, wg, wu, wd, j):
    D = hs[0].shape[-1]
    n_exp = wg.shape[1]
    h_all = jnp.concatenate([h.reshape(-1, D) for h in hs], axis=0)
    e_flat = jnp.concatenate([i.reshape(-1, LANES)[:, :TOP_K] for i in isels], axis=0).reshape(-1)
    n_assign = e_flat.shape[0]
    n_tiles = -(-n_assign // MOE_TILE) + n_exp
    n_rows = n_tiles * MOE_TILE

    blk = LANES
    n_blk = -(-n_assign // blk)
    onehot = (e_flat[:, None] == jnp.arange(n_exp, dtype=jnp.int32)[None, :]).astype(F32)
    oh3 = jnp.pad(onehot, ((0, n_blk * blk - n_assign), (0, 0))).reshape(n_blk, blk, n_exp)
    tril = (jnp.arange(blk)[:, None] >= jnp.arange(blk)[None, :]).astype(F32)
    within = jnp.einsum("ij,bje->bie", tril, oh3, precision=lax.Precision.HIGHEST)
    blk_tot = within[:, -1]
    blk_off = jnp.cumsum(blk_tot, axis=0) - blk_tot
    csum = (within + blk_off[:, None]).reshape(n_blk * blk, n_exp)[:n_assign]
    rank = jnp.sum(csum * onehot, axis=1).astype(jnp.int32) - 1
    counts = (blk_off[-1] + blk_tot[-1]).astype(jnp.int32)
    gsize = ((counts + MOE_TILE - 1) // MOE_TILE) * MOE_TILE
    gend = jnp.cumsum(gsize)
    gstart = gend - gsize
    pos = (gstart[e_flat] + rank).astype(jnp.int32)
    row_token = jnp.zeros((n_rows,), jnp.int32).at[pos].set(jnp.arange(n_assign, dtype=jnp.int32) // TOP_K)
    n_used = (gend[-1] // MOE_TILE).astype(jnp.int32)
    tile_start = jnp.arange(n_tiles, dtype=jnp.int32) * MOE_TILE
    tile_expert = jnp.sum((tile_start[:, None] >= gend[None, :]).astype(jnp.int32), axis=1)
    last_expert = jnp.sum((((n_used - 1) * MOE_TILE) >= gend).astype(jnp.int32))
    tile_expert = jnp.where(tile_start < gend[-1], tile_expert, last_expert).astype(jnp.int32)
    n_used1 = n_used.reshape(1)

    xg = _gather_rows(h_all, row_token, n_used1, n_rows)
    act = _grouped_up(xg, wg, wu, j, tile_expert, n_used1)
    og = _grouped_down(act, wd, j, tile_expert, n_used1)
    outs, base = [], 0
    for x, ws, gf in zip(xs, wsels, gates_f):
        outs.append(_combine(og, pos, ws, x, gf, base))
        base += x.shape[0] * x.shape[1]
    return outs


def _head_sum(x):
    s = x[:, 0:LANES]
    for i in range(1, MIX_WIDTH // LANES):
        s = s + x[:, i * LANES:(i + 1) * LANES]
    s = s + pltpu.roll(s, 64, axis=1)
    s = s + pltpu.roll(s, 32, axis=1)
    return s


def _tile16(s):
    return jnp.concatenate([s] * (MIX_WIDTH // LANES), axis=1)


def _rwkv_prep_kernel(z_ref, pz_ref, sh0_ref, mu_ref, wup_ref, w0_ref, aup_ref, a0_ref, gup_ref,
                      kk_ref, ka_ref, rk_ref,
                      a_out, wr_out, w_out, b_out, k_out, v_out, g_out, br_out, kr_out, rkr_out):
    s = pl.program_id(1)
    z = z_ref[0]
    ts = z.shape[0]
    prev = jnp.where(s == 0, sh0_ref[0], pz_ref[0][SUBLANES - 1:SUBLANES])
    rid = lax.broadcasted_iota(jnp.int32, z.shape, 0)
    zs = jnp.where(rid == 0, prev, pltpu.roll(z, 1, axis=0))
    zz = z + (zs - z) * mu_ref[0]
    r = zz[:, OFF_R:OFF_R + MIX_WIDTH]
    k = zz[:, OFF_K:OFF_K + MIX_WIDTH]
    v = zz[:, OFF_V:OFF_V + MIX_WIDTH]
    wl = jnp.tanh(zz[:, OFF_LORA:OFF_LORA + LANES])
    al = zz[:, OFF_LORA + LANES:OFF_LORA + 2 * LANES]
    gl = jax.nn.sigmoid(zz[:, OFF_LORA + 2 * LANES:OFF_LORA + 2 * LANES + GATE_LORA])

    w_raw = w0_ref[0] + jnp.dot(wl.astype(BF16), wup_ref[0].astype(BF16), preferred_element_type=F32)
    nx = -w_raw
    softplus = jnp.maximum(nx, 0.0) + jnp.log(1.0 + jnp.exp(-jnp.abs(nx)))
    w = jnp.exp(-jnp.exp(-softplus - 0.5))
    a = jax.nn.sigmoid(a0_ref[0] + jnp.dot(al.astype(BF16), aup_ref[0].astype(BF16),
                                           preferred_element_type=F32))
    g = jnp.dot(gl.astype(BF16), gup_ref[0].astype(BF16), preferred_element_type=F32)

    kk = k * kk_ref[0]
    nrm = jnp.sqrt(_head_sum(kk * kk))
    kk = kk * _tile16(1.0 / jnp.maximum(nrm, 1e-12))
    kt = k * (1.0 + (a - 1.0) * ka_ref[0])
    bs = kk * a

    a_out[0] = -kk
    wr_out[0] = w * r
    w_out[0] = w
    b_out[0] = bs
    k_out[0] = kt
    v_out[0] = v
    g_out[0] = g
    br_out[0] = _head_sum(bs * r)
    kr_out[0] = _head_sum(kt * r)
    rkr_out[0] = _head_sum(r * kt * rk_ref[0])


def _rwkv_prep(z_all, shift0, lw):
    B, S, NP = z_all.shape
    ts = _pick(S, (128, 64, 32, 16, 8))
    grid = (B, S // ts)
    nb8 = ts // SUBLANES
    vec = lambda n: pl.BlockSpec((1, 1, n), lambda b, s: (0, 0, 0))
    mat = lambda r, c: pl.BlockSpec((1, r, c), lambda b, s: (0, 0, 0))
    in_specs = [
        pl.BlockSpec((1, ts, RWKV_PAD), lambda b, s: (b, s, 0)),
        pl.BlockSpec((1, SUBLANES, RWKV_PAD), lambda b, s: (b, jnp.maximum(s * nb8 - 1, 0), 0)),
        pl.BlockSpec((1, 1, RWKV_PAD), lambda b, s: (b, 0, 0)),
        vec(RWKV_PAD),
        mat(LANES, MIX_WIDTH), vec(MIX_WIDTH),
        mat(LANES, MIX_WIDTH), vec(MIX_WIDTH),
        mat(GATE_LORA, MIX_WIDTH),
        vec(MIX_WIDTH), vec(MIX_WIDTH), vec(MIX_WIDTH),
    ]
    big = pl.BlockSpec((1, ts, MIX_WIDTH), lambda b, s: (b, s, 0))
    small = pl.BlockSpec((1, ts, LANES), lambda b, s: (b, s, 0))
    out_shape = [jax.ShapeDtypeStruct((B, S, MIX_WIDTH), F32)] * 7 + \
                [jax.ShapeDtypeStruct((B, S, LANES), F32)] * 3
    return pl.pallas_call(
        _rwkv_prep_kernel, out_shape=out_shape, grid=grid, in_specs=in_specs,
        out_specs=[big] * 7 + [small] * 3,
        compiler_params=_cparams(("parallel", "arbitrary")), name="rwkv_prep",
    )(z_all, z_all, shift0, lw["mu"], lw["w_up"], lw["w0"], lw["a_up"], lw["a0"], lw["g_up"],
      lw["k_k"], lw["k_a"], lw["r_k"])


N_VH = HEAD_DIM // SUBLANES
N_KH = MIX_WIDTH // LANES
N_STATE_TILES = N_VH * N_KH


def _lane_group_sum(x):
    return (x + pltpu.roll(x, 32, axis=1)) + (pltpu.roll(x, 64, axis=1) + pltpu.roll(x, 96, axis=1))


def _rwkv_scan_kernel(a_ref, wr_ref, w_ref, b_ref, k_ref, v_ref, br_ref, kr_ref, s0_ref,
                      y_ref, st_ref, S, bc, ybuf, *, tc):
    c = pl.program_id(1)

    @pl.when(c == 0)
    def _():
        S[...] = s0_ref[0]

    sub = lax.broadcasted_iota(jnp.int32, (SUBLANES, LANES), 0)
    lane_grp = lax.broadcasted_iota(jnp.int32, (SUBLANES, LANES), 1) >> 5
    diag = lane_grp == (sub & 3)
    low_half = sub < 4

    def token(base, jj):
        def row(ref, tile):
            return ref[0, pl.ds(base, SUBLANES), tile * LANES:(tile + 1) * LANES][jj:jj + 1]

        bcs = bc.at[jj % 2]
        for i, ref in enumerate((a_ref, wr_ref, w_ref, b_ref, k_ref)):
            for kh in range(N_KH):
                bcs[i * N_KH + kh] = jnp.broadcast_to(row(ref, kh), (SUBLANES, LANES))
        brr = row(br_ref, 0)
        krr = row(kr_ref, 0)

        def tree_sum(terms):
            while len(terms) > 1:
                terms = [terms[i] + terms[i + 1] for i in range(0, len(terms), 2)]
            return terms[0]

        for vh in range(N_VH):
            hs = [S[vh * N_KH + kh] for kh in range(N_KH)]
            u = _lane_group_sum(tree_sum([hs[kh] * bcs[kh] for kh in range(N_KH)]))
            o = _lane_group_sum(tree_sum([hs[kh] * bcs[N_KH + kh] for kh in range(N_KH)]))
            vsel = jnp.where(low_half, row(v_ref, 2 * vh), row(v_ref, 2 * vh + 1))
            vv = _lane_group_sum(jnp.where(diag, vsel, 0.0))
            y = o + u * brr + vv * krr
            yd = jnp.where(diag, y, 0.0)
            yd = yd + pltpu.roll(yd, 2, axis=0)
            yd = yd + pltpu.roll(yd, 1, axis=0)
            ybuf[jj:jj + 1, 2 * vh * LANES:(2 * vh + 1) * LANES] = yd[3:4]
            ybuf[jj:jj + 1, (2 * vh + 1) * LANES:(2 * vh + 2) * LANES] = yd[7:8]
            for kh in range(N_KH):
                S[vh * N_KH + kh] = (hs[kh] * bcs[2 * N_KH + kh] + bcs[3 * N_KH + kh] * u
                                     + bcs[4 * N_KH + kh] * vv)

    def group(gi, carry):
        base = pl.multiple_of(gi * SUBLANES, SUBLANES)
        for jj in range(SUBLANES):
            token(base, jj)
        y_ref[0, pl.ds(base, SUBLANES), :] = ybuf[...]
        return carry

    lax.fori_loop(0, tc // SUBLANES, group, 0)

    @pl.when(c == pl.num_programs(1) - 1)
    def _():
        st_ref[0] = S[...]


def _rwkv_scan(a, wr, w, b, k, v, br, kr, s0):
    B, S = a.shape[:2]
    tc = _pick(S, (128, 64, 32, 16, 8))
    grid = (B, S // tc)
    kspec = pl.BlockSpec((1, tc, MIX_WIDTH), lambda b, c: (b, c, 0))
    rspec = pl.BlockSpec((1, tc, LANES), lambda b, c: (b, c, 0))
    sspec = pl.BlockSpec((1, N_STATE_TILES, SUBLANES, LANES), lambda b, c: (b, 0, 0, 0))
    return pl.pallas_call(
        functools.partial(_rwkv_scan_kernel, tc=tc),
        out_shape=[jax.ShapeDtypeStruct((B, S, MIX_WIDTH), F32),
                   jax.ShapeDtypeStruct((B, N_STATE_TILES, SUBLANES, LANES), F32)],
        grid=grid,
        in_specs=[kspec] * 6 + [rspec, rspec, sspec],
        out_specs=[kspec, sspec],
        scratch_shapes=[pltpu.VMEM((N_STATE_TILES, SUBLANES, LANES), F32),
                        pltpu.VMEM((2, 5 * N_KH, SUBLANES, LANES), F32),
                        pltpu.VMEM((SUBLANES, MIX_WIDTH), F32)],
        compiler_params=_cparams(("parallel", "arbitrary")), name="rwkv_scan",
    )(a, wr, w, b, k, v, br, kr, s0)


def _rwkv_post_kernel(y_ref, v_ref, g_ref, rkr_ref, gw_ref, gb_ref, o_ref):
    y = y_ref[0]
    inv_n = 1.0 / HEAD_DIM
    d = y - _tile16(_head_sum(y) * inv_n)
    var = _head_sum(d * d) * inv_n
    yn = d * _tile16(lax.rsqrt(var + GN_EPS)) * gw_ref[0] + gb_ref[0]
    o_ref[0] = ((yn + _tile16(rkr_ref[0]) * v_ref[0]) * g_ref[0]).astype(o_ref.dtype)


def _rwkv_post(y, v, g, rkr, lw):
    B, S, _ = y.shape
    ts = _pick(S, (256, 128, 64, 32, 16, 8))
    big = pl.BlockSpec((1, ts, MIX_WIDTH), lambda b, s: (b, s, 0))
    small = pl.BlockSpec((1, ts, LANES), lambda b, s: (b, s, 0))
    vec = pl.BlockSpec((1, 1, MIX_WIDTH), lambda b, s: (0, 0, 0))
    return pl.pallas_call(
        _rwkv_post_kernel, out_shape=jax.ShapeDtypeStruct((B, S, MIX_WIDTH), BF16),
        grid=(B, S // ts), in_specs=[big, big, big, small, vec, vec], out_specs=big,
        compiler_params=_cparams(("parallel", "parallel")), name="rwkv_post",
    )(y, v, g, rkr, lw["gn_w"], lw["gn_b"])


def _conv_kernel(bg_ref, cg_ref, xin_ref, pcg_ref, pxin_ref, buf_ref, cw_ref, y_ref, st_ref):
    s = pl.program_id(2)
    u = cg_ref[0] * xin_ref[0]
    ts = u.shape[0]
    pu = pcg_ref[0] * pxin_ref[0]
    prev2 = jnp.where(s == 0, buf_ref[0], pu[SUBLANES - 2:SUBLANES])
    rid = lax.broadcasted_iota(jnp.int32, u.shape, 0)
    u1 = jnp.where(rid == 0, prev2[1:2], pltpu.roll(u, 1, axis=0))
    u2 = jnp.where(rid == 0, prev2[0:1], jnp.where(rid == 1, prev2[1:2], pltpu.roll(u, 2, axis=0)))
    cw = cw_ref[...]
    y = cw[0:1] * u2 + cw[1:2] * u1 + cw[2:3] * u
    y_ref[0] = (bg_ref[0] * y).astype(y_ref.dtype)

    @pl.when(s == pl.num_programs(2) - 1)
    def _():
        st_ref[0] = u[ts - (CONV_WIDTH - 1):ts]


def _conv(z_all, buf, conv_w, layer):
    B, S, _ = z_all.shape
    ts = _pick(S, (512, 256, 128, 64, 32, 16, 8))
    nb8 = ts // SUBLANES
    nj = MIX_WIDTH // COL_BLK
    c0 = OFF_CONV // COL_BLK
    blk = lambda off: pl.BlockSpec((1, ts, COL_BLK), lambda b, j, s, off=off: (b, s, off + j))
    pblk = lambda off: pl.BlockSpec(
        (1, SUBLANES, COL_BLK), lambda b, j, s, off=off: (b, jnp.maximum(s * nb8 - 1, 0), off + j))
    st_spec = pl.BlockSpec((1, CONV_WIDTH - 1, COL_BLK), lambda b, j, s: (b, 0, j))
    return pl.pallas_call(
        _conv_kernel,
        out_shape=[jax.ShapeDtypeStruct((B, S, MIX_WIDTH), BF16),
                   jax.ShapeDtypeStruct((B, CONV_WIDTH - 1, MIX_WIDTH), F32)],
        grid=(B, nj, S // ts),
        in_specs=[blk(c0), blk(c0 + nj), blk(c0 + 2 * nj), pblk(c0 + nj), pblk(c0 + 2 * nj), st_spec,
                  pl.BlockSpec((None, CONV_WIDTH, COL_BLK), lambda b, j, s: (layer, 0, j))],
        out_specs=[pl.BlockSpec((1, ts, COL_BLK), lambda b, j, s: (b, s, j)), st_spec],
        compiler_params=_cparams(("parallel", "parallel", "arbitrary")), name="gated_conv",
    )(z_all, z_all, z_all, z_all, z_all, buf, conv_w)


def _attn_core(q_refs, k_all, v_all, sinks_ref, q0, k0, o_ref):
    tq = q_refs[0].shape[1]
    tk = k_all.shape[0]
    rows = GROUP * tq
    tq_shift = tq.bit_length() - 1
    assert tq == 1 << tq_shift
    ri = lax.broadcasted_iota(jnp.int32, (rows, tk), 0)
    qpos = q0 + (ri & (tq - 1))
    kpos = k0 + lax.broadcasted_iota(jnp.int32, (rows, tk), 1)
    dc = (qpos >> CHUNK_SHIFT) - (kpos >> CHUNK_SHIFT)
    valid = (dc >= 0) & (dc <= W_CHUNKS) & (kpos >= 0)
    dist = jnp.abs(qpos - kpos).astype(F32)
    jrow = ri >> tq_shift
    jslope = jnp.where(jrow == 0, 2.0 ** -0.25, jnp.where(jrow == 1, 2.0 ** -0.5,
                                                         jnp.where(jrow == 2, 2.0 ** -0.75, 0.5)))
    bias0 = jnp.where(valid, -jslope * dist, -1e30)
    jcol = lax.broadcasted_iota(jnp.int32, (rows, 1), 0) >> tq_shift
    heads_per_blk = COL_BLK // HEAD_DIM
    for g in range(KV_HEADS):
        kg = k_all[:, g * HEAD_DIM:(g + 1) * HEAD_DIM].astype(BF16)
        vg = v_all[:, g * HEAD_DIM:(g + 1) * HEAD_DIM].astype(BF16)
        qs, sk = [], None
        for j in range(GROUP):
            h = g * GROUP + j
            lo = (h % heads_per_blk) * HEAD_DIM
            qs.append(q_refs[h // heads_per_blk][0, :, lo:lo + HEAD_DIM])
            sk = sinks_ref[h] if sk is None else jnp.where(jcol == j, sinks_ref[h], sk)
        qg = jnp.concatenate(qs, axis=0).astype(BF16)
        s = lax.dot_general(qg, kg, (((1,), (1,)), ((), ())), preferred_element_type=F32)
        s = s * ATTN_SCALE + bias0 * (2.0 ** -g)
        m = jnp.maximum(jnp.max(s, axis=-1, keepdims=True), sk)
        p = jnp.exp(s - m)
        den = jnp.sum(p, axis=-1, keepdims=True) + jnp.exp(sk - m)
        o = jnp.dot(p.astype(BF16), vg, preferred_element_type=F32) / den
        for j in range(GROUP):
            h = g * GROUP + j
            o_ref[0, :, h * HEAD_DIM:(h + 1) * HEAD_DIM] = o[j * tq:(j + 1) * tq].astype(o_ref.dtype)


def _attn_prompt_kernel(sinks_ref, q0_ref, q1_ref, q2_ref, q3_ref, ka_ref, kb_ref, kc_ref,
                        va_ref, vb_ref, vc_ref, o_ref):
    n = pl.program_id(1)
    k_all = jnp.concatenate([ka_ref[0], kb_ref[0], kc_ref[0]], axis=0)
    v_all = jnp.concatenate([va_ref[0], vb_ref[0], vc_ref[0]], axis=0)
    _attn_core((q0_ref, q1_ref, q2_ref, q3_ref), k_all, v_all, sinks_ref,
               n * CHUNK, (n - W_CHUNKS) * CHUNK, o_ref)


def _attn_prompt(z_all, sinks):
    B, S, _ = z_all.shape
    qb = OFF_ATTN // COL_BLK
    kb = qb + MIX_WIDTH // COL_BLK
    vb = kb + 1
    qspec = lambda i: pl.BlockSpec((1, CHUNK, COL_BLK), lambda b, n, s, i=i: (b, n, qb + i))
    wspec = lambda back, col: pl.BlockSpec(
        (1, CHUNK, COL_BLK), lambda b, n, s, back=back, col=col: (b, jnp.maximum(n - back, 0), col))
    grid_spec = pltpu.PrefetchScalarGridSpec(
        num_scalar_prefetch=1, grid=(B, S // CHUNK),
        in_specs=[qspec(0), qspec(1), qspec(2), qspec(3),
                  wspec(2, kb), wspec(1, kb), wspec(0, kb), wspec(2, vb), wspec(1, vb), wspec(0, vb)],
        out_specs=pl.BlockSpec((1, CHUNK, MIX_WIDTH), lambda b, n, s: (b, n, 0)))
    return pl.pallas_call(
        _attn_prompt_kernel, out_shape=jax.ShapeDtypeStruct((B, S, MIX_WIDTH), BF16),
        grid_spec=grid_spec, compiler_params=_cparams(("parallel", "parallel")), name="swa_prompt",
    )(sinks, *([z_all] * 10))


def _attn_sample_kernel(sinks_ref, q0_ref, q1_ref, q2_ref, q3_ref, kc_ref, kn_ref, vc_ref, vn_ref,
                        o_ref, *, pos0):
    L = kc_ref.shape[1]
    k_all = jnp.concatenate([kc_ref[0], kn_ref[0]], axis=0)
    v_all = jnp.concatenate([vc_ref[0], vn_ref[0]], axis=0)
    _attn_core((q0_ref, q1_ref, q2_ref, q3_ref), k_all, v_all, sinks_ref, pos0, pos0 - L, o_ref)


def _attn_sample(z_all, k_cache, v_cache, sinks, pos0):
    B, S, _ = z_all.shape
    L = k_cache.shape[1]
    qb = OFF_ATTN // COL_BLK
    kb = qb + MIX_WIDTH // COL_BLK
    vb = kb + 1
    qspec = lambda i: pl.BlockSpec((1, S, COL_BLK), lambda b, s, i=i: (b, 0, qb + i))
    nspec = lambda col: pl.BlockSpec((1, S, COL_BLK), lambda b, s, col=col: (b, 0, col))
    cspec = pl.BlockSpec((1, L, KV_WIDTH), lambda b, s: (b, 0, 0))
    grid_spec = pltpu.PrefetchScalarGridSpec(
        num_scalar_prefetch=1, grid=(B,),
        in_specs=[qspec(0), qspec(1), qspec(2), qspec(3), cspec, nspec(kb), cspec, nspec(vb)],
        out_specs=pl.BlockSpec((1, S, MIX_WIDTH), lambda b, s: (b, 0, 0)))
    return pl.pallas_call(
        functools.partial(_attn_sample_kernel, pos0=pos0),
        out_shape=jax.ShapeDtypeStruct((B, S, MIX_WIDTH), BF16),
        grid_spec=grid_spec, compiler_params=_cparams(("parallel",)), name="swa_sample",
    )(sinks, z_all, z_all, z_all, z_all, k_cache, z_all, v_cache, z_all)


def _merge_kernel(ya_ref, yb_ref, yc_ref, wa_ref, wb_ref, wc_ref, ga_ref, gb_ref, gc_ref, o_ref, *, tm):
    acc = None
    for y_ref, w_ref, g_ref in ((ya_ref, wa_ref, ga_ref), (yb_ref, wb_ref, gb_ref), (yc_ref, wc_ref, gc_ref)):
        y = y_ref[...].reshape(tm, y_ref.shape[-1])
        gate = jax.nn.sigmoid(g_ref[...].reshape(tm, g_ref.shape[-1]))
        c = gate * jnp.dot(y, w_ref[...].astype(BF16), preferred_element_type=F32)
        acc = c if acc is None else acc + c
    o_ref[...] = acc.reshape(o_ref.shape).astype(o_ref.dtype)


def _merge(ya, yb, yc, w0p, w_branch_out, layer, z_all, D):
    B, S, _ = ya.shape
    bb, ts = _row_tiles(B, S, 1024)
    tm = bb * ts
    tn = 256
    g0 = OFF_GATE // tn
    nd = D // tn
    yspec = pl.BlockSpec((bb, ts, MIX_WIDTH), lambda b, m, n: (b, m, 0))
    wspec = lambda r: pl.BlockSpec((None, None, MIX_WIDTH, tn), lambda b, m, n, r=r: (layer, r, 0, n))
    gspec = lambda r: pl.BlockSpec((bb, ts, tn), lambda b, m, n, r=r: (b, m, g0 + r * nd + n))
    return pl.pallas_call(
        functools.partial(_merge_kernel, tm=tm),
        out_shape=jax.ShapeDtypeStruct((B, S, D), BF16),
        grid=(B // bb, S // ts, nd),
        in_specs=[yspec, yspec, yspec,
                  pl.BlockSpec((MIX_WIDTH, tn), lambda b, m, n: (0, n)), wspec(1), wspec(2),
                  gspec(0), gspec(1), gspec(2)],
        out_specs=pl.BlockSpec((bb, ts, tn), lambda b, m, n: (b, m, n)),
        compiler_params=_cparams(("parallel", "parallel", "parallel")), name="branch_merge",
    )(ya, yb, yc, w0p, w_branch_out, w_branch_out, z_all, z_all, z_all)


def _im_cols(w):
    lead = w.shape[:-1]
    return jnp.swapaxes(w.reshape(lead + (N_HEADS, HEAD_DIM)), -1, -2).reshape(lead + (MIX_WIDTH,))


def _im_cols_inv(w):
    lead = w.shape[:-1]
    return jnp.swapaxes(w.reshape(lead + (HEAD_DIM, N_HEADS)), -1, -2).reshape(lead + (MIX_WIDTH,))


def _pack_rwkv_cols(t):
    lead = t.shape[:-1]
    z32 = jnp.zeros(lead + (LANES - DECAY_LORA,), t.dtype)
    o = 3 * MIX_WIDTH
    return jnp.concatenate([
        _im_cols(t[..., 0:MIX_WIDTH]), _im_cols(t[..., MIX_WIDTH:2 * MIX_WIDTH]),
        _im_cols(t[..., 2 * MIX_WIDTH:3 * MIX_WIDTH]),
        t[..., o:o + DECAY_LORA], z32,
        t[..., o + DECAY_LORA:o + DECAY_LORA + ICLR_LORA], z32,
        t[..., o + DECAY_LORA + ICLR_LORA:]], axis=-1)


def _unpack_rwkv_cols(t):
    o = OFF_LORA
    return jnp.concatenate([
        _im_cols_inv(t[..., 0:MIX_WIDTH]), _im_cols_inv(t[..., MIX_WIDTH:2 * MIX_WIDTH]),
        _im_cols_inv(t[..., 2 * MIX_WIDTH:3 * MIX_WIDTH]),
        t[..., o:o + DECAY_LORA], t[..., o + LANES:o + LANES + ICLR_LORA],
        t[..., o + 2 * LANES:o + 2 * LANES + GATE_LORA]], axis=-1)


def _pad_rows(w, rows):
    return jnp.concatenate([w, jnp.zeros((rows - w.shape[0],) + w.shape[1:], w.dtype)], axis=0)


def _state_to_tiles(s):
    B = s.shape[0]
    s = s.reshape(B, N_HEADS, N_VH, SUBLANES, N_KH, 4)
    s = jnp.transpose(s, (0, 2, 4, 3, 5, 1))
    return s.reshape(B, N_STATE_TILES, SUBLANES, LANES)


def _tiles_to_state(s):
    B = s.shape[0]
    s = s.reshape(B, N_VH, N_KH, SUBLANES, 4, N_HEADS)
    s = jnp.transpose(s, (0, 5, 1, 3, 2, 4))
    return s.reshape(B, N_HEADS, HEAD_DIM, HEAD_DIM)


def _mixers(i, x, m, st, pos0, lw, P):
    B, S, D = x.shape
    h = _norm(x, P["norm_mix_g"], i, m[1], m[0], name="norm_mix")
    z_all = _mm(h, [lw["w_all"]], [()], tm_max=1024, tn=1280, tk=D, name="in_proj")

    if st is None:
        s0 = jnp.zeros((B, N_STATE_TILES, SUBLANES, LANES), F32)
        shift0 = jnp.zeros((B, 1, RWKV_PAD), F32)
        conv0 = jnp.zeros((B, CONV_WIDTH - 1, MIX_WIDTH), F32)
    else:
        s0 = _state_to_tiles(st[0].astype(F32))
        shift0 = _pack_rwkv_cols(st[1].astype(F32))
        conv0 = st[2].astype(F32)

    a, wr, w, b, k, v, g, br, kr, rkr = _rwkv_prep(z_all, shift0, lw)
    y, s_tiles = _rwkv_scan(a, wr, w, b, k, v, br, kr, s0)
    y_a = _rwkv_post(y, v, g, rkr, lw)
    s_new = _tiles_to_state(s_tiles)
    shift_new = _unpack_rwkv_cols(z_all[:, S - 1:S, 0:RWKV_PAD])

    y_b, conv_new = _conv(z_all, conv0, P["conv_w"], i)

    k_new_all = z_all[:, :, OFF_ATTN + MIX_WIDTH:OFF_ATTN + MIX_WIDTH + KV_WIDTH]
    v_new_all = z_all[:, :, OFF_ATTN + MIX_WIDTH + KV_WIDTH:OFF_ATTN + MIX_WIDTH + 2 * KV_WIDTH]
    if st is None:
        y_c = _attn_prompt(z_all, lw["sinks"])
        k_new = k_new_all[:, S - WINDOW:].reshape(B, WINDOW, KV_HEADS, HEAD_DIM)
        v_new = v_new_all[:, S - WINDOW:].reshape(B, WINDOW, KV_HEADS, HEAD_DIM)
    else:
        L = st[3].shape[1]
        kc = st[3].astype(F32).reshape(B, L, KV_WIDTH)
        vc = st[4].astype(F32).reshape(B, L, KV_WIDTH)
        y_c = _attn_sample(z_all, kc, vc, lw["sinks"], pos0)
        k_new = jnp.concatenate([kc, k_new_all], axis=1)[:, -L:].reshape(B, L, KV_HEADS, HEAD_DIM)
        v_new = jnp.concatenate([vc, v_new_all], axis=1)[:, -L:].reshape(B, L, KV_HEADS, HEAD_DIM)

    merged = _merge(y_a, y_b, y_c, lw["wb0"], P["w_branch_out"], i, z_all, D)
    x = _mm(merged, [P["w_mix_out"]], [(i,)], tm_max=1024, tn=512, tk=D, epi="resgate",
            extras=(x, m[2]), name="mix_out")
    return x, (s_new, shift_new, conv_new, k_new, v_new)


def _channel_mixers(i, xs, ms, lw, P):
    j = i // 2
    if i % 2 == 0:
        outs = []
        for x, m in zip(xs, ms):
            h = _norm(x, P["norm_ffn_g"], i, m[4], m[3], name="norm_ffn")
            act = _mm(h, [P["ffn_w_gate"], P["ffn_w_up"]], [(j,), (j,)], tm_max=2048, tn=256,
                      tk=h.shape[-1], epi="swiglu", out_dtype=BF16, name="ffn_up")
            outs.append(_mm(act, [P["ffn_w_down"]], [(j,)], tm_max=1024, tn=1024, tk=2048, epi="resgate",
                            extras=(x, m[5]), name="ffn_down"))
        return outs
    n_exp = P["moe_w_gate"].shape[1]
    hs, wsels, isels = [], [], []
    for x, m in zip(xs, ms):
        h, logits = _norm(x, P["norm_ffn_g"], i, m[4], m[3], router=lw["router"], out_dtype=F32,
                          name="norm_moe")
        wsel, isel = _topk_route(logits, n_exp)
        hs.append(h)
        wsels.append(wsel)
        isels.append(isel)
    return _moe(hs, wsels, isels, xs, [m[5] for m in ms], P["moe_w_gate"], P["moe_w_up"],
                P["moe_w_down"], j)


def kernel(x_prompt, x_sample, c_prompt, c_sample, state_rwkv, state_rwkv_shift, state_conv, cache_swa_k, cache_swa_v, ada_w, ada_b, norm_mix_g, norm_ffn_g, w_in, rwkv_mu, rwkv_w_up, rwkv_w0, rwkv_a_up, rwkv_a0, rwkv_g_up, rwkv_k_k, rwkv_k_a, rwkv_r_k, rwkv_gn_w, rwkv_gn_b, conv_w, attn_sinks, w_branch_out, w_mix_out, ffn_w_gate, ffn_w_up, ffn_w_down, moe_router, moe_w_gate, moe_w_up, moe_w_down, final_norm_g):
    depth, D = norm_mix_g.shape
    Bp, Sp, _ = x_prompt.shape
    Bs, Ss, _ = x_sample.shape

    P = dict(norm_mix_g=norm_mix_g.reshape(depth, 1, D), norm_ffn_g=norm_ffn_g.reshape(depth, 1, D),
             conv_w=conv_w, w_branch_out=w_branch_out, w_mix_out=w_mix_out,
             ffn_w_gate=ffn_w_gate, ffn_w_up=ffn_w_up, ffn_w_down=ffn_w_down,
             moe_w_gate=moe_w_gate, moe_w_up=moe_w_up, moe_w_down=moe_w_down)

    nc = Bp + Bs
    nc_pad = -(-nc // 16) * 16
    c_all = jnp.concatenate([c_prompt, c_sample, jnp.zeros((nc_pad - nc, D), F32)], axis=0)[None]
    ada_b3 = ada_b.reshape(depth, 1, 6 * D)

    xp, xs = x_prompt, x_sample
    outs_p, outs_s = [], []
    for i in range(depth):
        mod = _mm(c_all, [ada_w], [(i,)], tm_max=2048, tn=1024, tk=2048, epi="bias", extras=(ada_b3, i),
                  silu_in=True, name="ada_mod")[0]
        mod_p = mod[:Bp].reshape(Bp, 6, D)
        mod_s = mod[Bp:nc].reshape(Bs, 6, D)
        ms_p = [mod_p[:, j:j + 1] for j in range(6)]
        ms_s = [mod_s[:, j:j + 1] for j in range(6)]

        o = 3 * MIX_WIDTH
        wi = w_in[i]
        zc = jnp.zeros((D, LANES - DECAY_LORA), F32)
        w_all = jnp.concatenate([
            _im_cols(wi[:, 0:MIX_WIDTH]), _im_cols(wi[:, MIX_WIDTH:2 * MIX_WIDTH]),
            _im_cols(wi[:, 2 * MIX_WIDTH:3 * MIX_WIDTH]),
            wi[:, o:o + DECAY_LORA], zc, wi[:, o + DECAY_LORA:o + DECAY_LORA + ICLR_LORA], zc,
            wi[:, o + DECAY_LORA + ICLR_LORA:]], axis=1).astype(BF16)
        vec = lambda t: _im_cols(t[i]).reshape(1, 1, MIX_WIDTH)
        lw = dict(
            w_all=w_all,
            mu=_pack_rwkv_cols(rwkv_mu[i]).reshape(1, 1, RWKV_PAD),
            w_up=_pad_rows(_im_cols(rwkv_w_up[i]), LANES)[None], w0=vec(rwkv_w0),
            a_up=_pad_rows(_im_cols(rwkv_a_up[i]), LANES)[None], a0=vec(rwkv_a0),
            g_up=_im_cols(rwkv_g_up[i])[None],
            k_k=vec(rwkv_k_k), k_a=vec(rwkv_k_a), r_k=vec(rwkv_r_k),
            gn_w=vec(rwkv_gn_w), gn_b=vec(rwkv_gn_b),
            sinks=attn_sinks[i],
            wb0=jnp.swapaxes(w_branch_out[i, 0].reshape(N_HEADS, HEAD_DIM, D), 0, 1).reshape(MIX_WIDTH, D),
        )
        if i % 2 == 1:
            r = moe_router[i // 2]
            lw["router"] = jnp.concatenate([r, jnp.zeros((D, LANES - r.shape[1]), F32)], axis=1)

        xp, st_p = _mixers(i, xp, ms_p, None, 0, lw, P)
        st_s_in = (state_rwkv[i], state_rwkv_shift[i], state_conv[i], cache_swa_k[i], cache_swa_v[i])
        xs, st_s = _mixers(i, xs, ms_s, st_s_in, PAST_LEN, lw, P)
        xp, xs = _channel_mixers(i, [xp, xs], [ms_p, ms_s], lw, P)
        outs_p.append(st_p)
        outs_s.append(st_s)

    fg = final_norm_g.reshape(1, 1, D)
    y_prompt = _norm(xp, fg, 0, out_dtype=F32, name="norm_final")
    y_sample = _norm(xs, fg, 0, out_dtype=F32, name="norm_final")
    stack = lambda outs, idx: jnp.stack([o[idx] for o in outs])
    return (y_prompt, y_sample,
            stack(outs_p, 0), stack(outs_p, 1), stack(outs_p, 2), stack(outs_p, 3), stack(outs_p, 4),
            stack(outs_s, 0), stack(outs_s, 1), stack(outs_s, 2), stack(outs_s, 3), stack(outs_s, 4))
```

```python
import functools
import math

import jax
import jax.numpy as jnp
from jax import lax
from jax.experimental import pallas as pl
from jax.experimental.pallas import tpu as pltpu

F32 = jnp.float32
BF16 = jnp.bfloat16

HEAD_DIM = 64
N_HEADS = 32
MIX_WIDTH = N_HEADS * HEAD_DIM
KV_HEADS = 8
GROUP = N_HEADS // KV_HEADS
KV_WIDTH = KV_HEADS * HEAD_DIM
CHUNK = 64
CHUNK_SHIFT = 6
W_CHUNKS = 2
WINDOW = W_CHUNKS * CHUNK
PAST_LEN = 1024
CONV_WIDTH = 3
DECAY_LORA = 96
ICLR_LORA = 96
GATE_LORA = 256
N_BRANCH = 3
TOP_K = 2
RMS_EPS = 1e-5
GN_EPS = 64e-5
ATTN_SCALE = HEAD_DIM ** -0.5

LANES = 128
SUBLANES = 8
VMEM_LIMIT_BYTES = 56 * 1024 * 1024

LORA_PAD = 512
OFF_R, OFF_K, OFF_V = 0, MIX_WIDTH, 2 * MIX_WIDTH
OFF_LORA = 3 * MIX_WIDTH
RWKV_PAD = OFF_LORA + LORA_PAD
OFF_CONV = RWKV_PAD
OFF_ATTN = OFF_CONV + 3 * MIX_WIDTH
OFF_GATE = OFF_ATTN + MIX_WIDTH + 2 * KV_WIDTH
COL_BLK = 512
RWKV_COLS = 3 * MIX_WIDTH + DECAY_LORA + ICLR_LORA + GATE_LORA


def _cparams(sem):
    return pltpu.CompilerParams(dimension_semantics=sem, vmem_limit_bytes=VMEM_LIMIT_BYTES)


def _pick(n, cands):
    for c in cands:
        if n % c == 0:
            return c
    return n


def _row_tiles(B, S, tm_max):
    if S >= tm_max:
        ts = _pick(S, (tm_max, tm_max // 2, tm_max // 4, 256, 128, 64, 32, 16, 8))
        return 1, ts
    bb = max(1, min(B, tm_max // S))
    while B % bb:
        bb -= 1
    return bb, S


def _mm_kernel(*refs, n_w, epi, tm, nk, silu_in):
    x_ref = refs[0]
    w_refs = refs[1:1 + n_w]
    pos = 1 + n_w
    n_extra = {"none": 0, "swiglu": 0, "bias": 1, "resgate": 2}[epi]
    extra = refs[pos:pos + n_extra]
    pos += n_extra
    o_ref = refs[pos]
    acc_refs = refs[pos + 1:]

    x = x_ref[...]
    x = x.reshape(tm, x.shape[-1])
    if silu_in:
        xf = x.astype(F32)
        x = xf * jax.nn.sigmoid(xf)
    x = x.astype(BF16)

    def finish(accs):
        a = accs[0]
        if epi == "swiglu":
            r = (a * jax.nn.sigmoid(a)) * accs[1]
        elif epi == "bias":
            r = a + extra[0][...]
        elif epi == "resgate":
            r = extra[0][...] + extra[1][...] * a.reshape(o_ref.shape)
        else:
            r = a
        o_ref[...] = r.reshape(o_ref.shape).astype(o_ref.dtype)

    dots = [jnp.dot(x, w_refs[j][...].astype(BF16), preferred_element_type=F32) for j in range(n_w)]
    if nk == 1:
        finish(dots)
        return

    k = pl.program_id(3)
    if not acc_refs:
        acc_refs = (o_ref,)

    @pl.when(k == 0)
    def _():
        for acc in acc_refs:
            acc[...] = jnp.zeros_like(acc)

    for j in range(n_w):
        acc_refs[j][...] += dots[j].reshape(acc_refs[j].shape)

    if epi != "none" or acc_refs[0] is not o_ref:
        @pl.when(k == nk - 1)
        def _():
            finish([acc[...] for acc in acc_refs])


def _mm(x, ws, leads, *, tm_max, tn, tk, epi="none", out_dtype=F32, extras=(), silu_in=False,
        name="mm"):
    B, S, K = x.shape
    N = ws[0].shape[-1]
    bb, ts = _row_tiles(B, S, tm_max)
    tm = bb * ts
    tn = _pick(N, (tn, 512, 256, 128))
    tk = _pick(K, (tk, 1024, 512, 256, 128))
    nk = K // tk
    grid = (B // bb, S // ts, N // tn, nk)

    in_specs = [pl.BlockSpec((bb, ts, tk), lambda b, m, n, k: (b, m, k))]
    for w, lead in zip(ws, leads):
        nl = len(lead)
        in_specs.append(pl.BlockSpec((None,) * nl + (tk, tn),
                                     lambda b, m, n, k, lead=lead: tuple(lead) + (k, n)))
    if epi == "bias":
        in_specs.append(pl.BlockSpec((None, 1, tn), lambda b, m, n, k, l=extras[1]: (l, 0, n)))
        extras = extras[:1]
    elif epi == "resgate":
        in_specs.append(pl.BlockSpec((bb, ts, tn), lambda b, m, n, k: (b, m, n)))
        in_specs.append(pl.BlockSpec((bb, 1, tn), lambda b, m, n, k: (b, 0, n)))
    n_w = len(ws)
    direct = nk == 1 or (n_w == 1 and out_dtype == F32)
    kern = functools.partial(_mm_kernel, n_w=n_w, epi=epi, tm=tm, nk=nk, silu_in=silu_in)
    return pl.pallas_call(
        kern,
        out_shape=jax.ShapeDtypeStruct((B, S, N), out_dtype),
        grid=grid,
        in_specs=in_specs,
        out_specs=pl.BlockSpec((bb, ts, tn), lambda b, m, n, k: (b, m, n)),
        scratch_shapes=[] if direct else [pltpu.VMEM((tm, tn), F32) for _ in range(n_w)],
        compiler_params=_cparams(("parallel", "parallel", "parallel", "arbitrary")),
        name=name,
    )(x, *ws, *extras)


def _norm_kernel(*refs, modulate, route):
    x_ref, g_ref = refs[0], refs[1]
    pos = 2
    if modulate:
        sc_ref, sh_ref = refs[2], refs[3]
        pos = 4
    if route:
        r_ref = refs[pos]
        pos += 1
    o_ref = refs[pos]
    x = x_ref[...]
    ms = jnp.mean(x * x, axis=-1, keepdims=True)
    y = x * lax.rsqrt(ms + RMS_EPS) * g_ref[...]
    if modulate:
        y = y * (1.0 + sc_ref[...]) + sh_ref[...]
    o_ref[...] = y.astype(o_ref.dtype)
    if route:
        l_ref = refs[pos + 1]
        y2 = y.reshape(-1, y.shape[-1])
        logits = jnp.dot(y2, r_ref[...], preferred_element_type=F32, precision=lax.Precision.HIGHEST)
        l_ref[...] = logits.reshape(l_ref.shape)


def _norm(x, g3, lead, scale=None, shift=None, router=None, out_dtype=BF16, name="norm"):
    B, S, D = x.shape
    bb, ts = _row_tiles(B, S, 256)
    grid = (B // bb, S // ts)
    modulate = scale is not None
    route = router is not None
    in_specs = [pl.BlockSpec((bb, ts, D), lambda b, m: (b, m, 0)),
                pl.BlockSpec((None, 1, D), lambda b, m: (lead, 0, 0))]
    args = [x, g3]
    if modulate:
        in_specs += [pl.BlockSpec((bb, 1, D), lambda b, m: (b, 0, 0))] * 2
        args += [scale, shift]
    out_shape = [jax.ShapeDtypeStruct((B, S, D), out_dtype)]
    out_specs = [pl.BlockSpec((bb, ts, D), lambda b, m: (b, m, 0))]
    if route:
        in_specs.append(pl.BlockSpec((D, LANES), lambda b, m: (0, 0)))
        args.append(router)
        out_shape.append(jax.ShapeDtypeStruct((B, S, LANES), F32))
        out_specs.append(pl.BlockSpec((bb, ts, LANES), lambda b, m: (b, m, 0)))
    res = pl.pallas_call(
        functools.partial(_norm_kernel, modulate=modulate, route=route),
        out_shape=out_shape, grid=grid, in_specs=in_specs, out_specs=out_specs,
        compiler_params=_cparams(("parallel", "parallel")), name=name,
    )(*args)
    return res if route else res[0]


def _topk_kernel(l_ref, w_ref, i_ref, *, n_exp):
    l = l_ref[...]
    lane = lax.broadcasted_iota(jnp.int32, l.shape, l.ndim - 1)
    neg = jnp.float32(-jnp.inf)
    l = jnp.where(lane < n_exp, l, neg)
    m1 = jnp.max(l, axis=-1, keepdims=True)
    i1 = jnp.min(jnp.where(l == m1, lane, LANES), axis=-1, keepdims=True)
    l2 = jnp.where(lane == i1, neg, l)
    m2 = jnp.max(l2, axis=-1, keepdims=True)
    i2 = jnp.min(jnp.where(l2 == m2, lane, LANES), axis=-1, keepdims=True)
    e = jnp.exp(m2 - m1)
    den = 1.0 + e
    w_ref[...] = jnp.where(lane == 0, 1.0 / den, jnp.where(lane == 1, e / den, 0.0))
    i_ref[...] = jnp.where(lane == 0, i1, jnp.where(lane == 1, i2, 0))


def _topk_route(logits, n_exp):
    B, S, _ = logits.shape
    bb, ts = _row_tiles(B, S, 1024)
    spec = pl.BlockSpec((bb, ts, LANES), lambda b, m: (b, m, 0))
    return pl.pallas_call(
        functools.partial(_topk_kernel, n_exp=n_exp),
        out_shape=[jax.ShapeDtypeStruct(logits.shape, F32), jax.ShapeDtypeStruct(logits.shape, jnp.int32)],
        grid=(B // bb, S // ts), in_specs=[spec], out_specs=[spec, spec],
        compiler_params=_cparams(("parallel", "parallel")), name="topk_route",
    )(logits)


MOE_TILE = 1024
GATHER_ROWS = 256


def _row_copy(src_hbm, dst_vmem, sem, src_row, dst_row):
    return pltpu.make_async_copy(src_hbm.at[pl.ds(src_row, 1), :], dst_vmem.at[pl.ds(dst_row, 1), :], sem)


def _gather_kernel(tok_ref, nu_ref, h_hbm, o_ref, buf, sem, *, rows):
    i = pl.program_id(0)
    n_live_rows = nu_ref[0] * MOE_TILE

    def live(step):
        return jnp.logical_and(step < pl.num_programs(0), step * rows < n_live_rows)

    def start_all(step):
        slot = step % 2
        base = step * rows

        def body(r2, c):
            for p in range(2):
                r = r2 * 2 + p
                _row_copy(h_hbm, buf.at[slot], sem.at[slot], tok_ref[base + r], r).start(priority=p)
            return c

        lax.fori_loop(0, rows // 2, body, 0, unroll=4)

    @pl.when(i == 0)
    def _():
        start_all(i)

    @pl.when(live(i + 1))
    def _():
        start_all(i + 1)

    @pl.when(live(i))
    def _():
        slot = i % 2

        def wait(r, c):
            _row_copy(h_hbm, buf.at[slot], sem.at[slot], 0, r).wait()
            return c

        lax.fori_loop(0, rows, wait, 0, unroll=8)
        o_ref[...] = buf[slot].astype(o_ref.dtype)

    @pl.when(jnp.logical_not(live(i)))
    def _():
        o_ref[...] = jnp.zeros_like(o_ref)


def _gather_rows(h, row_token, n_used, n_rows):
    T, D = h.shape
    rows = GATHER_ROWS
    grid_spec = pltpu.PrefetchScalarGridSpec(
        num_scalar_prefetch=2, grid=(n_rows // rows,),
        in_specs=[pl.BlockSpec(memory_space=pl.ANY)],
        out_specs=pl.BlockSpec((rows, D), lambda i, tok, nu: (i, 0)),
        scratch_shapes=[pltpu.VMEM((2, rows, D), F32), pltpu.SemaphoreType.DMA((2,))])
    return pl.pallas_call(
        functools.partial(_gather_kernel, rows=rows),
        out_shape=jax.ShapeDtypeStruct((n_rows, D), BF16), grid_spec=grid_spec,
        compiler_params=_cparams(("arbitrary",)), name="moe_gather",
    )(row_token, n_used, h)


def _gup_kernel(te_ref, nu_ref, x_ref, wg_ref, wu_ref, o_ref):
    m = pl.program_id(0)

    @pl.when(m < nu_ref[0])
    def _():
        x = x_ref[...]
        g = jnp.dot(x, wg_ref[...].astype(BF16), preferred_element_type=F32)
        u = jnp.dot(x, wu_ref[...].astype(BF16), preferred_element_type=F32)
        o_ref[...] = ((g * jax.nn.sigmoid(g)) * u).astype(o_ref.dtype)

    @pl.when(m >= nu_ref[0])
    def _():
        o_ref[...] = jnp.zeros_like(o_ref)


def _grouped_up(xg, wg, wu, j, tile_expert, n_used):
    P, D = xg.shape
    F = wg.shape[-1]
    tm = MOE_TILE
    tn = _pick(F, (256, 128))
    nn = F // tn

    def wmap(m, n, te, nu):
        return (j, te[m], 0, jnp.where(m < nu[0], n, nn - 1))

    grid_spec = pltpu.PrefetchScalarGridSpec(
        num_scalar_prefetch=2, grid=(P // tm, nn),
        in_specs=[pl.BlockSpec((tm, D), lambda m, n, te, nu: (jnp.minimum(m, nu[0] - 1), 0)),
                  pl.BlockSpec((None, None, D, tn), wmap), pl.BlockSpec((None, None, D, tn), wmap)],
        out_specs=pl.BlockSpec((tm, tn), lambda m, n, te, nu: (m, n)))
    return pl.pallas_call(
        _gup_kernel, out_shape=jax.ShapeDtypeStruct((P, F), BF16), grid_spec=grid_spec,
        compiler_params=_cparams(("arbitrary", "arbitrary")), name="moe_up",
    )(tile_expert, n_used, xg, wg, wu)


def _gdown_kernel(te_ref, nu_ref, x_ref, w_ref, o_ref):
    m = pl.program_id(0)
    k = pl.program_id(2)

    @pl.when(k == 0)
    def _():
        o_ref[...] = jnp.zeros_like(o_ref)

    @pl.when(m < nu_ref[0])
    def _():
        o_ref[...] += jnp.dot(x_ref[...], w_ref[...].astype(BF16), preferred_element_type=F32)


def _grouped_down(act, wd, j, tile_expert, n_used):
    P, F = act.shape
    D = wd.shape[-1]
    tm = MOE_TILE
    tn = _pick(D, (2048, 1024, 512, 256, 128))
    tk = _pick(F, (1024, 512, 256, 128))
    nn, nk = D // tn, F // tk

    def live(m, nu, idx, last):
        return jnp.where(m < nu[0], idx, last)

    grid_spec = pltpu.PrefetchScalarGridSpec(
        num_scalar_prefetch=2, grid=(P // tm, nn, nk),
        in_specs=[pl.BlockSpec((tm, tk), lambda m, n, k, te, nu: (jnp.minimum(m, nu[0] - 1), live(m, nu, k, nk - 1))),
                  pl.BlockSpec((None, None, tk, tn),
                               lambda m, n, k, te, nu: (j, te[m], live(m, nu, k, nk - 1), live(m, nu, n, nn - 1)))],
        out_specs=pl.BlockSpec((tm, tn), lambda m, n, k, te, nu: (m, n)))
    return pl.pallas_call(
        _gdown_kernel, out_shape=jax.ShapeDtypeStruct((P, D), F32), grid_spec=grid_spec,
        compiler_params=_cparams(("arbitrary", "arbitrary", "arbitrary")), name="moe_down",
    )(tile_expert, n_used, act, wd)


def _combine_kernel(pos_ref, og_hbm, w_ref, x_ref, g_ref, o_ref, buf, sem, *, rows, base):
    t0 = base + (pl.program_id(0) * pl.num_programs(1) + pl.program_id(1)) * rows

    def start(r, c):
        for jj in range(TOP_K):
            _row_copy(og_hbm, buf.at[jj], sem, pos_ref[TOP_K * (t0 + r) + jj], r).start(priority=jj)
        return c

    def wait(r, c):
        for jj in range(TOP_K):
            _row_copy(og_hbm, buf.at[jj], sem, 0, r).wait()
        return c

    lax.fori_loop(0, rows, start, 0)
    lax.fori_loop(0, rows, wait, 0)
    w = w_ref[0]
    f = w[:, 0:1] * buf[0] + w[:, 1:2] * buf[1]
    o_ref[0] = x_ref[0] + g_ref[0] * f


def _combine(og, pos, wsel, x, gate_f, base):
    B, S, D = x.shape
    rows = _pick(S, (256, 128, 64, 32, 16, 8))
    grid_spec = pltpu.PrefetchScalarGridSpec(
        num_scalar_prefetch=1, grid=(B, S // rows),
        in_specs=[pl.BlockSpec(memory_space=pl.ANY),
                  pl.BlockSpec((1, rows, LANES), lambda b, s, p: (b, s, 0)),
                  pl.BlockSpec((1, rows, D), lambda b, s, p: (b, s, 0)),
                  pl.BlockSpec((1, 1, D), lambda b, s, p: (b, 0, 0))],
        out_specs=pl.BlockSpec((1, rows, D), lambda b, s, p: (b, s, 0)),
        scratch_shapes=[pltpu.VMEM((TOP_K, rows, D), F32), pltpu.SemaphoreType.DMA])
    return pl.pallas_call(
        functools.partial(_combine_kernel, rows=rows, base=base),
        out_shape=jax.ShapeDtypeStruct((B, S, D), F32), grid_spec=grid_spec,
        compiler_params=_cparams(("arbitrary", "arbitrary")), name="moe_combine",
    )(pos, og, wsel, x, gate_f)


def _moe(hs, wsels, isels, xs, gates_f, wg, wu, wd, j):
    D = hs[0].shape[-1]
    n_exp = wg.shape[1]
    h_all = jnp.concatenate([h.reshape(-1, D) for h in hs], axis=0)
    e_flat = jnp.concatenate([i.reshape(-1, LANES)[:, :TOP_K] for i in isels], axis=0).reshape(-1)
    n_assign = e_flat.shape[0]
    n_tiles = -(-n_assign // MOE_TILE) + n_exp
    n_rows = n_tiles * MOE_TILE

    blk = LANES
    n_blk = -(-n_assign // blk)
    onehot = (e_flat[:, None] == jnp.arange(n_exp, dtype=jnp.int32)[None, :]).astype(F32)
    oh3 = jnp.pad(onehot, ((0, n_blk * blk - n_assign), (0, 0))).reshape(n_blk, blk, n_exp)
    tril = (jnp.arange(blk)[:, None] >= jnp.arange(blk)[None, :]).astype(F32)
    within = jnp.einsum("ij,bje->bie", tril, oh3, precision=lax.Precision.HIGHEST)
    blk_tot = within[:, -1]
    blk_off = jnp.cumsum(blk_tot, axis=0) - blk_tot
    csum = (within + blk_off[:, None]).reshape(n_blk * blk, n_exp)[:n_assign]
    rank = jnp.sum(csum * onehot, axis=1).astype(jnp.int32) - 1
    counts = (blk_off[-1] + blk_tot[-1]).astype(jnp.int32)
    gsize = ((counts + MOE_TILE - 1) // MOE_TILE) * MOE_TILE
    gend = jnp.cumsum(gsize)
    gstart = gend - gsize
    pos = (gstart[e_flat] + rank).astype(jnp.int32)
    row_token = jnp.zeros((n_rows,), jnp.int32).at[pos].set(jnp.arange(n_assign, dtype=jnp.int32) // TOP_K)
    n_used = (gend[-1] // MOE_TILE).astype(jnp.int32)
    tile_start = jnp.arange(n_tiles, dtype=jnp.int32) * MOE_TILE
    tile_expert = jnp.sum((tile_start[:, None] >= gend[None, :]).astype(jnp.int32), axis=1)
    last_expert = jnp.sum((((n_used - 1) * MOE_TILE) >= gend).astype(jnp.int32))
    tile_expert = jnp.where(tile_start < gend[-1], tile_expert, last_expert).astype(jnp.int32)
    n_used1 = n_used.reshape(1)

    xg = _gather_rows(h_all, row_token, n_used1, n_rows)
    act = _grouped_up(xg, wg, wu, j, tile_expert, n_used1)
    og = _grouped_down(act, wd, j, tile_expert, n_used1)
    outs, base = [], 0
    for x, ws, gf in zip(xs, wsels, gates_f):
        outs.append(_combine(og, pos, ws, x, gf, base))
        base += x.shape[0] * x.shape[1]
    return outs


def _head_sum(x):
    s = x[:, 0:LANES]
    for i in range(1, MIX_WIDTH // LANES):
        s = s + x[:, i * LANES:(i + 1) * LANES]
    s = s + pltpu.roll(s, 64, axis=1)
    s = s + pltpu.roll(s, 32, axis=1)
    return s


def _tile16(s):
    return jnp.concatenate([s] * (MIX_WIDTH // LANES), axis=1)


def _rwkv_prep_kernel(z_ref, pz_ref, sh0_ref, mu_ref, wup_ref, w0_ref, aup_ref, a0_ref, gup_ref,
                      kk_ref, ka_ref, rk_ref,
                      a_out, wr_out, w_out, b_out, k_out, v_out, g_out, br_out, kr_out, rkr_out):
    s = pl.program_id(1)
    z = z_ref[0]
    ts = z.shape[0]
    prev = jnp.where(s == 0, sh0_ref[0], pz_ref[0][SUBLANES - 1:SUBLANES])
    rid = lax.broadcasted_iota(jnp.int32, z.shape, 0)
    zs = jnp.where(rid == 0, prev, pltpu.roll(z, 1, axis=0))
    zz = z + (zs - z) * mu_ref[0]
    r = zz[:, OFF_R:OFF_R + MIX_WIDTH]
    k = zz[:, OFF_K:OFF_K + MIX_WIDTH]
    v = zz[:, OFF_V:OFF_V + MIX_WIDTH]
    wl = jnp.tanh(zz[:, OFF_LORA:OFF_LORA + LANES])
    al = zz[:, OFF_LORA + LANES:OFF_LORA + 2 * LANES]
    gl = jax.nn.sigmoid(zz[:, OFF_LORA + 2 * LANES:OFF_LORA + 2 * LANES + GATE_LORA])

    w_raw = w0_ref[0] + jnp.dot(wl.astype(BF16), wup_ref[0].astype(BF16), preferred_element_type=F32)
    nx = -w_raw
    softplus = jnp.maximum(nx, 0.0) + jnp.log(1.0 + jnp.exp(-jnp.abs(nx)))
    w = jnp.exp(-jnp.exp(-softplus - 0.5))
    a = jax.nn.sigmoid(a0_ref[0] + jnp.dot(al.astype(BF16), aup_ref[0].astype(BF16),
                                           preferred_element_type=F32))
    g = jnp.dot(gl.astype(BF16), gup_ref[0].astype(BF16), preferred_element_type=F32)

    kk = k * kk_ref[0]
    nrm = jnp.sqrt(_head_sum(kk * kk))
    kk = kk * _tile16(1.0 / jnp.maximum(nrm, 1e-12))
    kt = k * (1.0 + (a - 1.0) * ka_ref[0])
    bs = kk * a

    a_out[0] = -kk
    wr_out[0] = w * r
    w_out[0] = w
    b_out[0] = bs
    k_out[0] = kt
    v_out[0] = v
    g_out[0] = g
    br_out[0] = _head_sum(bs * r)
    kr_out[0] = _head_sum(kt * r)
    rkr_out[0] = _head_sum(r * kt * rk_ref[0])


def _rwkv_prep(z_all, shift0, lw):
    B, S, NP = z_all.shape
    ts = _pick(S, (128, 64, 32, 16, 8))
    grid = (B, S // ts)
    nb8 = ts // SUBLANES
    vec = lambda n: pl.BlockSpec((1, 1, n), lambda b, s: (0, 0, 0))
    mat = lambda r, c: pl.BlockSpec((1, r, c), lambda b, s: (0, 0, 0))
    in_specs = [
        pl.BlockSpec((1, ts, RWKV_PAD), lambda b, s: (b, s, 0)),
        pl.BlockSpec((1, SUBLANES, RWKV_PAD), lambda b, s: (b, jnp.maximum(s * nb8 - 1, 0), 0)),
        pl.BlockSpec((1, 1, RWKV_PAD), lambda b, s: (b, 0, 0)),
        vec(RWKV_PAD),
        mat(LANES, MIX_WIDTH), vec(MIX_WIDTH),
        mat(LANES, MIX_WIDTH), vec(MIX_WIDTH),
        mat(GATE_LORA, MIX_WIDTH),
        vec(MIX_WIDTH), vec(MIX_WIDTH), vec(MIX_WIDTH),
    ]
    big = pl.BlockSpec((1, ts, MIX_WIDTH), lambda b, s: (b, s, 0))
    small = pl.BlockSpec((1, ts, LANES), lambda b, s: (b, s, 0))
    out_shape = [jax.ShapeDtypeStruct((B, S, MIX_WIDTH), F32)] * 7 + \
                [jax.ShapeDtypeStruct((B, S, LANES), F32)] * 3
    return pl.pallas_call(
        _rwkv_prep_kernel, out_shape=out_shape, grid=grid, in_specs=in_specs,
        out_specs=[big] * 7 + [small] * 3,
        compiler_params=_cparams(("parallel", "arbitrary")), name="rwkv_prep",
    )(z_all, z_all, shift0, lw["mu"], lw["w_up"], lw["w0"], lw["a_up"], lw["a0"], lw["g_up"],
      lw["k_k"], lw["k_a"], lw["r_k"])


N_VH = HEAD_DIM // SUBLANES
N_KH = MIX_WIDTH // LANES
N_STATE_TILES = N_VH * N_KH


def _lane_group_sum(x):
    return (x + pltpu.roll(x, 32, axis=1)) + (pltpu.roll(x, 64, axis=1) + pltpu.roll(x, 96, axis=1))


def _rwkv_scan_kernel(a_ref, wr_ref, w_ref, b_ref, k_ref, v_ref, br_ref, kr_ref, s0_ref,
                      y_ref, st_ref, S, bc, ybuf, *, tc):
    c = pl.program_id(1)

    @pl.when(c == 0)
    def _():
        S[...] = s0_ref[0]

    sub = lax.broadcasted_iota(jnp.int32, (SUBLANES, LANES), 0)
    lane_grp = lax.broadcasted_iota(jnp.int32, (SUBLANES, LANES), 1) >> 5
    diag = lane_grp == (sub & 3)
    low_half = sub < 4

    def token(base, jj):
        def row(ref, tile):
            return ref[0, pl.ds(base, SUBLANES), tile * LANES:(tile + 1) * LANES][jj:jj + 1]

        bcs = bc.at[jj % 2]
        for i, ref in enumerate((a_ref, wr_ref, w_ref, b_ref, k_ref)):
            for kh in range(N_KH):
                bcs[i * N_KH + kh] = jnp.broadcast_to(row(ref, kh), (SUBLANES, LANES))
        brr = row(br_ref, 0)
        krr = row(kr_ref, 0)

        def tree_sum(terms):
            while len(terms) > 1:
                terms = [terms[i] + terms[i + 1] for i in range(0, len(terms), 2)]
            return terms[0]

        for vh in range(N_VH):
            hs = [S[vh * N_KH + kh] for kh in range(N_KH)]
            u = _lane_group_sum(tree_sum([hs[kh] * bcs[kh] for kh in range(N_KH)]))
            o = _lane_group_sum(tree_sum([hs[kh] * bcs[N_KH + kh] for kh in range(N_KH)]))
            vsel = jnp.where(low_half, row(v_ref, 2 * vh), row(v_ref, 2 * vh + 1))
            vv = _lane_group_sum(jnp.where(diag, vsel, 0.0))
            y = o + u * brr + vv * krr
            yd = jnp.where(diag, y, 0.0)
            yd = yd + pltpu.roll(yd, 2, axis=0)
            yd = yd + pltpu.roll(yd, 1, axis=0)
            ybuf[jj:jj + 1, 2 * vh * LANES:(2 * vh + 1) * LANES] = yd[3:4]
            ybuf[jj:jj + 1, (2 * vh + 1) * LANES:(2 * vh + 2) * LANES] = yd[7:8]
            for kh in range(N_KH):
                S[vh * N_KH + kh] = (hs[kh] * bcs[2 * N_KH + kh] + bcs[3 * N_KH + kh] * u
                                     + bcs[4 * N_KH + kh] * vv)

    def group(gi, carry):
        base = pl.multiple_of(gi * SUBLANES, SUBLANES)
        for jj in range(SUBLANES):
            token(base, jj)
        y_ref[0, pl.ds(base, SUBLANES), :] = ybuf[...]
        return carry

    lax.fori_loop(0, tc // SUBLANES, group, 0)

    @pl.when(c == pl.num_programs(1) - 1)
    def _():
        st_ref[0] = S[...]


def _rwkv_scan(a, wr, w, b, k, v, br, kr, s0):
    B, S = a.shape[:2]
    tc = _pick(S, (128, 64, 32, 16, 8))
    grid = (B, S // tc)
    kspec = pl.BlockSpec((1, tc, MIX_WIDTH), lambda b, c: (b, c, 0))
    rspec = pl.BlockSpec((1, tc, LANES), lambda b, c: (b, c, 0))
    sspec = pl.BlockSpec((1, N_STATE_TILES, SUBLANES, LANES), lambda b, c: (b, 0, 0, 0))
    return pl.pallas_call(
        functools.partial(_rwkv_scan_kernel, tc=tc),
        out_shape=[jax.ShapeDtypeStruct((B, S, MIX_WIDTH), F32),
                   jax.ShapeDtypeStruct((B, N_STATE_TILES, SUBLANES, LANES), F32)],
        grid=grid,
        in_specs=[kspec] * 6 + [rspec, rspec, sspec],
        out_specs=[kspec, sspec],
        scratch_shapes=[pltpu.VMEM((N_STATE_TILES, SUBLANES, LANES), F32),
                        pltpu.VMEM((2, 5 * N_KH, SUBLANES, LANES), F32),
                        pltpu.VMEM((SUBLANES, MIX_WIDTH), F32)],
        compiler_params=_cparams(("parallel", "arbitrary")), name="rwkv_scan",
    )(a, wr, w, b, k, v, br, kr, s0)


def _rwkv_post_kernel(y_ref, v_ref, g_ref, rkr_ref, gw_ref, gb_ref, o_ref):
    y = y_ref[0]
    inv_n = 1.0 / HEAD_DIM
    d = y - _tile16(_head_sum(y) * inv_n)
    var = _head_sum(d * d) * inv_n
    yn = d * _tile16(lax.rsqrt(var + GN_EPS)) * gw_ref[0] + gb_ref[0]
    o_ref[0] = ((yn + _tile16(rkr_ref[0]) * v_ref[0]) * g_ref[0]).astype(o_ref.dtype)


def _rwkv_post(y, v, g, rkr, lw):
    B, S, _ = y.shape
    ts = _pick(S, (256, 128, 64, 32, 16, 8))
    big = pl.BlockSpec((1, ts, MIX_WIDTH), lambda b, s: (b, s, 0))
    small = pl.BlockSpec((1, ts, LANES), lambda b, s: (b, s, 0))
    vec = pl.BlockSpec((1, 1, MIX_WIDTH), lambda b, s: (0, 0, 0))
    return pl.pallas_call(
        _rwkv_post_kernel, out_shape=jax.ShapeDtypeStruct((B, S, MIX_WIDTH), BF16),
        grid=(B, S // ts), in_specs=[big, big, big, small, vec, vec], out_specs=big,
        compiler_params=_cparams(("parallel", "parallel")), name="rwkv_post",
    )(y, v, g, rkr, lw["gn_w"], lw["gn_b"])


def _conv_kernel(bg_ref, cg_ref, xin_ref, pcg_ref, pxin_ref, buf_ref, cw_ref, y_ref, st_ref):
    s = pl.program_id(2)
    u = cg_ref[0] * xin_ref[0]
    ts = u.shape[0]
    pu = pcg_ref[0] * pxin_ref[0]
    prev2 = jnp.where(s == 0, buf_ref[0], pu[SUBLANES - 2:SUBLANES])
    rid = lax.broadcasted_iota(jnp.int32, u.shape, 0)
    u1 = jnp.where(rid == 0, prev2[1:2], pltpu.roll(u, 1, axis=0))
    u2 = jnp.where(rid == 0, prev2[0:1], jnp.where(rid == 1, prev2[1:2], pltpu.roll(u, 2, axis=0)))
    cw = cw_ref[...]
    y = cw[0:1] * u2 + cw[1:2] * u1 + cw[2:3] * u
    y_ref[0] = (bg_ref[0] * y).astype(y_ref.dtype)

    @pl.when(s == pl.num_programs(2) - 1)
    def _():
        st_ref[0] = u[ts - (CONV_WIDTH - 1):ts]


def _conv(z_all, buf, conv_w, layer):
    B, S, _ = z_all.shape
    ts = _pick(S, (512, 256, 128, 64, 32, 16, 8))
    nb8 = ts // SUBLANES
    nj = MIX_WIDTH // COL_BLK
    c0 = OFF_CONV // COL_BLK
    blk = lambda off: pl.BlockSpec((1, ts, COL_BLK), lambda b, j, s, off=off: (b, s, off + j))
    pblk = lambda off: pl.BlockSpec(
        (1, SUBLANES, COL_BLK), lambda b, j, s, off=off: (b, jnp.maximum(s * nb8 - 1, 0), off + j))
    st_spec = pl.BlockSpec((1, CONV_WIDTH - 1, COL_BLK), lambda b, j, s: (b, 0, j))
    return pl.pallas_call(
        _conv_kernel,
        out_shape=[jax.ShapeDtypeStruct((B, S, MIX_WIDTH), BF16),
                   jax.ShapeDtypeStruct((B, CONV_WIDTH - 1, MIX_WIDTH), F32)],
        grid=(B, nj, S // ts),
        in_specs=[blk(c0), blk(c0 + nj), blk(c0 + 2 * nj), pblk(c0 + nj), pblk(c0 + 2 * nj), st_spec,
                  pl.BlockSpec((None, CONV_WIDTH, COL_BLK), lambda b, j, s: (layer, 0, j))],
        out_specs=[pl.BlockSpec((1, ts, COL_BLK), lambda b, j, s: (b, s, j)), st_spec],
        compiler_params=_cparams(("parallel", "parallel", "arbitrary")), name="gated_conv",
    )(z_all, z_all, z_all, z_all, z_all, buf, conv_w)


def _attn_core(q_refs, k_all, v_all, sinks_ref, q0, k0, o_ref):
    tq = q_refs[0].shape[1]
    tk = k_all.shape[0]
    rows = GROUP * tq
    tq_shift = tq.bit_length() - 1
    assert tq == 1 << tq_shift
    ri = lax.broadcasted_iota(jnp.int32, (rows, tk), 0)
    qpos = q0 + (ri & (tq - 1))
    kpos = k0 + lax.broadcasted_iota(jnp.int32, (rows, tk), 1)
    dc = (qpos >> CHUNK_SHIFT) - (kpos >> CHUNK_SHIFT)
    valid = (dc >= 0) & (dc <= W_CHUNKS) & (kpos >= 0)
    dist = jnp.abs(qpos - kpos).astype(F32)
    jrow = ri >> tq_shift
    jslope = jnp.where(jrow == 0, 2.0 ** -0.25, jnp.where(jrow == 1, 2.0 ** -0.5,
                                                         jnp.where(jrow == 2, 2.0 ** -0.75, 0.5)))
    bias0 = jnp.where(valid, -jslope * dist, -1e30)
    jcol = lax.broadcasted_iota(jnp.int32, (rows, 1), 0) >> tq_shift
    heads_per_blk = COL_BLK // HEAD_DIM
    for g in range(KV_HEADS):
        kg = k_all[:, g * HEAD_DIM:(g + 1) * HEAD_DIM].astype(BF16)
        vg = v_all[:, g * HEAD_DIM:(g + 1) * HEAD_DIM].astype(BF16)
        qs, sk = [], None
        for j in range(GROUP):
            h = g * GROUP + j
            lo = (h % heads_per_blk) * HEAD_DIM
            qs.append(q_refs[h // heads_per_blk][0, :, lo:lo + HEAD_DIM])
            sk = sinks_ref[h] if sk is None else jnp.where(jcol == j, sinks_ref[h], sk)
        qg = jnp.concatenate(qs, axis=0).astype(BF16)
        s = lax.dot_general(qg, kg, (((1,), (1,)), ((), ())), preferred_element_type=F32)
        s = s * ATTN_SCALE + bias0 * (2.0 ** -g)
        m = jnp.maximum(jnp.max(s, axis=-1, keepdims=True), sk)
        p = jnp.exp(s - m)
        den = jnp.sum(p, axis=-1, keepdims=True) + jnp.exp(sk - m)
        o = jnp.dot(p.astype(BF16), vg, preferred_element_type=F32) / den
        for j in range(GROUP):
            h = g * GROUP + j
            o_ref[0, :, h * HEAD_DIM:(h + 1) * HEAD_DIM] = o[j * tq:(j + 1) * tq].astype(o_ref.dtype)


def _attn_prompt_kernel(sinks_ref, q0_ref, q1_ref, q2_ref, q3_ref, ka_ref, kb_ref, kc_ref,
                        va_ref, vb_ref, vc_ref, o_ref):
    n = pl.program_id(1)
    k_all = jnp.concatenate([ka_ref[0], kb_ref[0], kc_ref[0]], axis=0)
    v_all = jnp.concatenate([va_ref[0], vb_ref[0], vc_ref[0]], axis=0)
    _attn_core((q0_ref, q1_ref, q2_ref, q3_ref), k_all, v_all, sinks_ref,
               n * CHUNK, (n - W_CHUNKS) * CHUNK, o_ref)


def _attn_prompt(z_all, sinks):
    B, S, _ = z_all.shape
    qb = OFF_ATTN // COL_BLK
    kb = qb + MIX_WIDTH // COL_BLK
    vb = kb + 1
    qspec = lambda i: pl.BlockSpec((1, CHUNK, COL_BLK), lambda b, n, s, i=i: (b, n, qb + i))
    wspec = lambda back, col: pl.BlockSpec(
        (1, CHUNK, COL_BLK), lambda b, n, s, back=back, col=col: (b, jnp.maximum(n - back, 0), col))
    grid_spec = pltpu.PrefetchScalarGridSpec(
        num_scalar_prefetch=1, grid=(B, S // CHUNK),
        in_specs=[qspec(0), qspec(1), qspec(2), qspec(3),
                  wspec(2, kb), wspec(1, kb), wspec(0, kb), wspec(2, vb), wspec(1, vb), wspec(0, vb)],
        out_specs=pl.BlockSpec((1, CHUNK, MIX_WIDTH), lambda b, n, s: (b, n, 0)))
    return pl.pallas_call(
        _attn_prompt_kernel, out_shape=jax.ShapeDtypeStruct((B, S, MIX_WIDTH), BF16),
        grid_spec=grid_spec, compiler_params=_cparams(("parallel", "parallel")), name="swa_prompt",
    )(sinks, *([z_all] * 10))


def _attn_sample_kernel(sinks_ref, q0_ref, q1_ref, q2_ref, q3_ref, kc_ref, kn_ref, vc_ref, vn_ref,
                        o_ref, *, pos0):
    L = kc_ref.shape[1]
    k_all = jnp.concatenate([kc_ref[0], kn_ref[0]], axis=0)
    v_all = jnp.concatenate([vc_ref[0], vn_ref[0]], axis=0)
    _attn_core((q0_ref, q1_ref, q2_ref, q3_ref), k_all, v_all, sinks_ref, pos0, pos0 - L, o_ref)


def _attn_sample(z_all, k_cache, v_cache, sinks, pos0):
    B, S, _ = z_all.shape
    L = k_cache.shape[1]
    qb = OFF_ATTN // COL_BLK
    kb = qb + MIX_WIDTH // COL_BLK
    vb = kb + 1
    qspec = lambda i: pl.BlockSpec((1, S, COL_BLK), lambda b, s, i=i: (b, 0, qb + i))
    nspec = lambda col: pl.BlockSpec((1, S, COL_BLK), lambda b, s, col=col: (b, 0, col))
    cspec = pl.BlockSpec((1, L, KV_WIDTH), lambda b, s: (b, 0, 0))
    grid_spec = pltpu.PrefetchScalarGridSpec(
        num_scalar_prefetch=1, grid=(B,),
        in_specs=[qspec(0), qspec(1), qspec(2), qspec(3), cspec, nspec(kb), cspec, nspec(vb)],
        out_specs=pl.BlockSpec((1, S, MIX_WIDTH), lambda b, s: (b, 0, 0)))
    return pl.pallas_call(
        functools.partial(_attn_sample_kernel, pos0=pos0),
        out_shape=jax.ShapeDtypeStruct((B, S, MIX_WIDTH), BF16),
        grid_spec=grid_spec, compiler_params=_cparams(("parallel",)), name="swa_sample",
    )(sinks, z_all, z_all, z_all, z_all, k_cache, z_all, v_cache, z_all)


def _merge_kernel(ya_ref, yb_ref, yc_ref, wa_ref, wb_ref, wc_ref, ga_ref, gb_ref, gc_ref, o_ref, *, tm):
    acc = None
    for y_ref, w_ref, g_ref in ((ya_ref, wa_ref, ga_ref), (yb_ref, wb_ref, gb_ref), (yc_ref, wc_ref, gc_ref)):
        y = y_ref[...].reshape(tm, y_ref.shape[-1])
        gate = jax.nn.sigmoid(g_ref[...].reshape(tm, g_ref.shape[-1]))
        c = gate * jnp.dot(y, w_ref[...].astype(BF16), preferred_element_type=F32)
        acc = c if acc is None else acc + c
    o_ref[...] = acc.reshape(o_ref.shape).astype(o_ref.dtype)


def _merge(ya, yb, yc, w0p, w_branch_out, layer, z_all, D):
    B, S, _ = ya.shape
    bb, ts = _row_tiles(B, S, 1024)
    tm = bb * ts
    tn = 256
    g0 = OFF_GATE // tn
    nd = D // tn
    yspec = pl.BlockSpec((bb, ts, MIX_WIDTH), lambda b, m, n: (b, m, 0))
    wspec = lambda r: pl.BlockSpec((None, None, MIX_WIDTH, tn), lambda b, m, n, r=r: (layer, r, 0, n))
    gspec = lambda r: pl.BlockSpec((bb, ts, tn), lambda b, m, n, r=r: (b, m, g0 + r * nd + n))
    return pl.pallas_call(
        functools.partial(_merge_kernel, tm=tm),
        out_shape=jax.ShapeDtypeStruct((B, S, D), BF16),
        grid=(B // bb, S // ts, nd),
        in_specs=[yspec, yspec, yspec,
                  pl.BlockSpec((MIX_WIDTH, tn), lambda b, m, n: (0, n)), wspec(1), wspec(2),
                  gspec(0), gspec(1), gspec(2)],
        out_specs=pl.BlockSpec((bb, ts, tn), lambda b, m, n: (b, m, n)),
        compiler_params=_cparams(("parallel", "parallel", "parallel")), name="branch_merge",
    )(ya, yb, yc, w0p, w_branch_out, w_branch_out, z_all, z_all, z_all)


def _im_cols(w):
    lead = w.shape[:-1]
    return jnp.swapaxes(w.reshape(lead + (N_HEADS, HEAD_DIM)), -1, -2).reshape(lead + (MIX_WIDTH,))


def _im_cols_inv(w):
    lead = w.shape[:-1]
    return jnp.swapaxes(w.reshape(lead + (HEAD_DIM, N_HEADS)), -1, -2).reshape(lead + (MIX_WIDTH,))


def _pack_rwkv_cols(t):
    lead = t.shape[:-1]
    z32 = jnp.zeros(lead + (LANES - DECAY_LORA,), t.dtype)
    o = 3 * MIX_WIDTH
    return jnp.concatenate([
        _im_cols(t[..., 0:MIX_WIDTH]), _im_cols(t[..., MIX_WIDTH:2 * MIX_WIDTH]),
        _im_cols(t[..., 2 * MIX_WIDTH:3 * MIX_WIDTH]),
        t[..., o:o + DECAY_LORA], z32,
        t[..., o + DECAY_LORA:o + DECAY_LORA + ICLR_LORA], z32,
        t[..., o + DECAY_LORA + ICLR_LORA:]], axis=-1)


def _unpack_rwkv_cols(t):
    o = OFF_LORA
    return jnp.concatenate([
        _im_cols_inv(t[..., 0:MIX_WIDTH]), _im_cols_inv(t[..., MIX_WIDTH:2 * MIX_WIDTH]),
        _im_cols_inv(t[..., 2 * MIX_WIDTH:3 * MIX_WIDTH]),
        t[..., o:o + DECAY_LORA], t[..., o + LANES:o + LANES + ICLR_LORA],
        t[..., o + 2 * LANES:o + 2 * LANES + GATE_LORA]], axis=-1)


def _pad_rows(w, rows):
    return jnp.concatenate([w, jnp.zeros((rows - w.shape[0],) + w.shape[1:], w.dtype)], axis=0)


def _state_to_tiles(s):
    B = s.shape[0]
    s = s.reshape(B, N_HEADS, N_VH, SUBLANES, N_KH, 4)
    s = jnp.transpose(s, (0, 2, 4, 3, 5, 1))
    return s.reshape(B, N_STATE_TILES, SUBLANES, LANES)


def _tiles_to_state(s):
    B = s.shape[0]
    s = s.reshape(B, N_VH, N_KH, SUBLANES, 4, N_HEADS)
    s = jnp.transpose(s, (0, 5, 1, 3, 2, 4))
    return s.reshape(B, N_HEADS, HEAD_DIM, HEAD_DIM)


def _mixers(i, x, m, st, pos0, lw, P):
    B, S, D = x.shape
    h = _norm(x, P["norm_mix_g"], i, m[1], m[0], name="norm_mix")
    z_all = _mm(h, [lw["w_all"]], [()], tm_max=1024, tn=1280, tk=D, name="in_proj")

    if st is None:
        s0 = jnp.zeros((B, N_STATE_TILES, SUBLANES, LANES), F32)
        shift0 = jnp.zeros((B, 1, RWKV_PAD), F32)
        conv0 = jnp.zeros((B, CONV_WIDTH - 1, MIX_WIDTH), F32)
    else:
        s0 = _state_to_tiles(st[0].astype(F32))
        shift0 = _pack_rwkv_cols(st[1].astype(F32))
        conv0 = st[2].astype(F32)

    a, wr, w, b, k, v, g, br, kr, rkr = _rwkv_prep(z_all, shift0, lw)
    y, s_tiles = _rwkv_scan(a, wr, w, b, k, v, br, kr, s0)
    y_a = _rwkv_post(y, v, g, rkr, lw)
    s_new = _tiles_to_state(s_tiles)
    shift_new = _unpack_rwkv_cols(z_all[:, S - 1:S, 0:RWKV_PAD])

    y_b, conv_new = _conv(z_all, conv0, P["conv_w"], i)

    k_new_all = z_all[:, :, OFF_ATTN + MIX_WIDTH:OFF_ATTN + MIX_WIDTH + KV_WIDTH]
    v_new_all = z_all[:, :, OFF_ATTN + MIX_WIDTH + KV_WIDTH:OFF_ATTN + MIX_WIDTH + 2 * KV_WIDTH]
    if st is None:
        y_c = _attn_prompt(z_all, lw["sinks"])
        k_new = k_new_all[:, S - WINDOW:].reshape(B, WINDOW, KV_HEADS, HEAD_DIM)
        v_new = v_new_all[:, S - WINDOW:].reshape(B, WINDOW, KV_HEADS, HEAD_DIM)
    else:
        L = st[3].shape[1]
        kc = st[3].astype(F32).reshape(B, L, KV_WIDTH)
        vc = st[4].astype(F32).reshape(B, L, KV_WIDTH)
        y_c = _attn_sample(z_all, kc, vc, lw["sinks"], pos0)
        k_new = jnp.concatenate([kc, k_new_all], axis=1)[:, -L:].reshape(B, L, KV_HEADS, HEAD_DIM)
        v_new = jnp.concatenate([vc, v_new_all], axis=1)[:, -L:].reshape(B, L, KV_HEADS, HEAD_DIM)

    merged = _merge(y_a, y_b, y_c, lw["wb0"], P["w_branch_out"], i, z_all, D)
    x = _mm(merged, [P["w_mix_out"]], [(i,)], tm_max=1024, tn=512, tk=D, epi="resgate",
            extras=(x, m[2]), name="mix_out")
    return x, (s_new, shift_new, conv_new, k_new, v_new)


def _channel_mixers(i, xs, ms, lw, P):
    j = i // 2
    if i % 2 == 0:
        outs = []
        for x, m in zip(xs, ms):
            h = _norm(x, P["norm_ffn_g"], i, m[4], m[3], name="norm_ffn")
            act = _mm(h, [P["ffn_w_gate"], P["ffn_w_up"]], [(j,), (j,)], tm_max=2048, tn=256,
                      tk=h.shape[-1], epi="swiglu", out_dtype=BF16, name="ffn_up")
            outs.append(_mm(act, [P["ffn_w_down"]], [(j,)], tm_max=1024, tn=1024, tk=2048, epi="resgate",
                            extras=(x, m[5]), name="ffn_down"))
        return outs
    n_exp = P["moe_w_gate"].shape[1]
    hs, wsels, isels = [], [], []
    for x, m in zip(xs, ms):
        h, logits = _norm(x, P["norm_ffn_g"], i, m[4], m[3], router=lw["router"], out_dtype=F32,
                          name="norm_moe")
        wsel, isel = _topk_route(logits, n_exp)
        hs.append(h)
        wsels.append(wsel)
        isels.append(isel)
    return _moe(hs, wsels, isels, xs, [m[5] for m in ms], P["moe_w_gate"], P["moe_w_up"],
                P["moe_w_down"], j)


def kernel(x_prompt, x_sample, c_prompt, c_sample, state_rwkv, state_rwkv_shift, state_conv, cache_swa_k, cache_swa_v, ada_w, ada_b, norm_mix_g, norm_ffn_g, w_in, rwkv_mu, rwkv_w_up, rwkv_w0, rwkv_a_up, rwkv_a0, rwkv_g_up, rwkv_k_k, rwkv_k_a, rwkv_r_k, rwkv_gn_w, rwkv_gn_b, conv_w, attn_sinks, w_branch_out, w_mix_out, ffn_w_gate, ffn_w_up, ffn_w_down, moe_router, moe_w_gate, moe_w_up, moe_w_down, final_norm_g):
    depth, D = norm_mix_g.shape
    Bp, Sp, _ = x_prompt.shape
    Bs, Ss, _ = x_sample.shape

    P = dict(norm_mix_g=norm_mix_g.reshape(depth, 1, D), norm_ffn_g=norm_ffn_g.reshape(depth, 1, D),
             conv_w=conv_w, w_branch_out=w_branch_out, w_mix_out=w_mix_out,
             ffn_w_gate=ffn_w_gate, ffn_w_up=ffn_w_up, ffn_w_down=ffn_w_down,
             moe_w_gate=moe_w_gate, moe_w_up=moe_w_up, moe_w_down=moe_w_down)

    nc = Bp + Bs
    nc_pad = -(-nc // 16) * 16
    c_all = jnp.concatenate([c_prompt, c_sample, jnp.zeros((nc_pad - nc, D), F32)], axis=0)[None]
    ada_b3 = ada_b.reshape(depth, 1, 6 * D)

    xp, xs = x_prompt, x_sample
    outs_p, outs_s = [], []
    for i in range(depth):
        mod = _mm(c_all, [ada_w], [(i,)], tm_max=2048, tn=1024, tk=2048, epi="bias", extras=(ada_b3, i),
                  silu_in=True, name="ada_mod")[0]
        mod_p = mod[:Bp].reshape(Bp, 6, D)
        mod_s = mod[Bp:nc].reshape(Bs, 6, D)
        ms_p = [mod_p[:, j:j + 1] for j in range(6)]
        ms_s = [mod_s[:, j:j + 1] for j in range(6)]

        o = 3 * MIX_WIDTH
        wi = w_in[i]
        zc = jnp.zeros((D, LANES - DECAY_LORA), F32)
        w_all = jnp.concatenate([
            _im_cols(wi[:, 0:MIX_WIDTH]), _im_cols(wi[:, MIX_WIDTH:2 * MIX_WIDTH]),
            _im_cols(wi[:, 2 * MIX_WIDTH:3 * MIX_WIDTH]),
            wi[:, o:o + DECAY_LORA], zc, wi[:, o + DECAY_LORA:o + DECAY_LORA + ICLR_LORA], zc,
            wi[:, o + DECAY_LORA + ICLR_LORA:]], axis=1).astype(BF16)
        vec = lambda t: _im_cols(t[i]).reshape(1, 1, MIX_WIDTH)
        lw = dict(
            w_all=w_all,
            mu=_pack_rwkv_cols(rwkv_mu[i]).reshape(1, 1, RWKV_PAD),
            w_up=_pad_rows(_im_cols(rwkv_w_up[i]), LANES)[None], w0=vec(rwkv_w0),
            a_up=_pad_rows(_im_cols(rwkv_a_up[i]), LANES)[None], a0=vec(rwkv_a0),
            g_up=_im_cols(rwkv_g_up[i])[None],
            k_k=vec(rwkv_k_k), k_a=vec(rwkv_k_a), r_k=vec(rwkv_r_k),
            gn_w=vec(rwkv_gn_w), gn_b=vec(rwkv_gn_b),
            sinks=attn_sinks[i],
            wb0=jnp.swapaxes(w_branch_out[i, 0].reshape(N_HEADS, HEAD_DIM, D), 0, 1).reshape(MIX_WIDTH, D),
        )
        if i % 2 == 1:
            r = moe_router[i // 2]
            lw["router"] = jnp.concatenate([r, jnp.zeros((D, LANES - r.shape[1]), F32)], axis=1)

        xp, st_p = _mixers(i, xp, ms_p, None, 0, lw, P)
        st_s_in = (state_rwkv[i], state_rwkv_shift[i], state_conv[i], cache_swa_k[i], cache_swa_v[i])
        xs, st_s = _mixers(i, xs, ms_s, st_s_in, PAST_LEN, lw, P)
        xp, xs = _channel_mixers(i, [xp, xs], [ms_p, ms_s], lw, P)
        outs_p.append(st_p)
        outs_s.append(st_s)

    fg = final_norm_g.reshape(1, 1, D)
    y_prompt = _norm(xp, fg, 0, out_dtype=F32, name="norm_final")
    y_sample = _norm(xs, fg, 0, out_dtype=F32, name="norm_final")
    stack = lambda outs, idx: jnp.stack([o[idx] for o in outs])
    return (y_prompt, y_sample,
            stack(outs_p, 0), stack(outs_p, 1), stack(outs_p, 2), stack(outs_p, 3), stack(outs_p, 4),
            stack(outs_s, 0), stack(outs_s, 1), stack(outs_s, 2), stack(outs_s, 3), stack(outs_s, 4))
```
